```python
import math
import jax, jax.numpy as jnp
from jax import lax
import numpy as np

D_MODEL = 1024
BATCH = 2
SEQ = 8192
DEPTH = 2

N_HEADS = D_MODEL // 128
HEAD_DIM = 64
ATTN_W = N_HEADS * 2 * HEAD_DIM
CONV_CH = D_MODEL
CONV_K = 31
D_FF = ((8 * D_MODEL // 3 + 255) // 256) * 256
NUM_BUCKETS = 32
MAX_DISTANCE = 128
Q_BLOCK = 128
N_BRANCH = 2
MIX_IN = 3 * ATTN_W + 2 * CONV_CH + N_BRANCH * D_MODEL
N_MOD = 9
EPS = 1e-6

kernel_name = "hybrid_diffattn_conformer_conv_macaron_adaln"


def rms_norm(x, g):
    xf = x.astype(jnp.float32)
    y = xf * lax.rsqrt(jnp.mean(xf * xf, axis=-1, keepdims=True) + EPS)
    return (y * g.astype(jnp.float32)).astype(x.dtype)


def layer_norm(x, g, b):
    xf = x.astype(jnp.float32)
    mu = jnp.mean(xf, axis=-1, keepdims=True)
    xc = xf - mu
    y = xc * lax.rsqrt(jnp.mean(xc * xc, axis=-1, keepdims=True) + EPS)
    return (y * g.astype(jnp.float32) + b.astype(jnp.float32)).astype(x.dtype)


def modulate(h, shift, scale):
    return h * (1 + scale[:, None, :]) + shift[:, None, :]


def swiglu(h, w_in, w_out):
    g, u = jnp.split(h @ w_in, 2, axis=-1)
    return (jax.nn.silu(g) * u) @ w_out


def t5_bucket(dist):
    n = jnp.maximum(dist, 0)
    max_exact = NUM_BUCKETS // 2
    nf = jnp.maximum(n, 1).astype(jnp.float32)
    large = max_exact + (jnp.log(nf / max_exact) / math.log(MAX_DISTANCE / max_exact)
                         * (NUM_BUCKETS - max_exact)).astype(jnp.int32)
    large = jnp.minimum(large, NUM_BUCKETS - 1)
    return jnp.where(n < max_exact, n, large)


def diff_attention(q, k, v, lam, rel_bias):
    B, S = q.shape[0], q.shape[1]
    nb = S // Q_BLOCK
    qb = q.reshape(B, nb, Q_BLOCK, N_HEADS, 2, HEAD_DIM).swapaxes(0, 1)
    kpos = jnp.arange(S)
    scale = HEAD_DIM ** -0.5
    table = rel_bias.astype(jnp.float32)
    neg = jnp.finfo(jnp.float32).min

    def block(args):
        qblk, i = args
        qpos = i * Q_BLOCK + jnp.arange(Q_BLOCK)
        dist = qpos[:, None] - kpos[None, :]
        bias = jnp.take(table, t5_bucket(dist), axis=1)
        s = jnp.einsum('bqhmd,bkhmd->bhmqk', qblk, k).astype(jnp.float32) * scale
        s = s + bias[None, :, None]
        s = jnp.where(dist >= 0, s, neg)
        p = jax.nn.softmax(s, axis=-1)
        w = p[:, :, 0] - lam * p[:, :, 1]
        return jnp.einsum('bhqk,bkhe->bqhe', w.astype(v.dtype), v)

    o = lax.map(block, (qb, jnp.arange(nb)))
    return o.swapaxes(0, 1).reshape(B, S, N_HEADS, 2 * HEAD_DIM)


def conformer_conv(u, w_dw, b_dw, ln_g, ln_b, w_out):
    a, g = jnp.split(u, 2, axis=-1)
    z = a * jax.nn.sigmoid(g)
    z = lax.conv_general_dilated(z, w_dw[:, None, :], window_strides=(1,),
                                 padding=[(CONV_K - 1, 0)],
                                 dimension_numbers=('NWC', 'WIO', 'NWC'),
                                 feature_group_count=CONV_CH) + b_dw
    z = jax.nn.silu(layer_norm(z, ln_g, ln_b))
    return z @ w_out


def setup_inputs(seed: int = 0) -> dict:
    key = jax.random.key(seed)
    ks = jax.random.split(key, 32)

    def nrm(k, shape, scale):
        return jax.random.normal(k, shape, jnp.float32) * scale

    def gain(k, shape):
        return 1.0 + 0.05 * jax.random.normal(k, shape, jnp.float32)

    L, D = DEPTH, D_MODEL
    return {
        "x": nrm(ks[0], (BATCH, SEQ, D), 1.0),
        "c": nrm(ks[1], (BATCH, D), 1.0),
        "ffn1_norm": gain(ks[2], (L, D)),
        "ffn1_w_in": nrm(ks[3], (L, D, 2 * D_FF), D ** -0.5),
        "ffn1_w_out": nrm(ks[4], (L, D_FF, D), D_FF ** -0.5),
        "mix_norm": gain(ks[5], (L, D)),
        "w_mix_in": nrm(ks[6], (L, D, MIX_IN), D ** -0.5),
        "lambda_q1": nrm(ks[7], (L, HEAD_DIM), 0.1),
        "lambda_k1": nrm(ks[8], (L, HEAD_DIM), 0.1),
        "lambda_q2": nrm(ks[9], (L, HEAD_DIM), 0.1),
        "lambda_k2": nrm(ks[10], (L, HEAD_DIM), 0.1),
        "subln_gain": gain(ks[11], (L, 2 * HEAD_DIM)),
        "w_attn_branch": nrm(ks[12], (L, ATTN_W, D), ATTN_W ** -0.5),
        "conv_dw": nrm(ks[13], (L, CONV_K, CONV_CH), CONV_K ** -0.5),
        "conv_dw_bias": nrm(ks[14], (L, CONV_CH), 0.02),
        "conv_ln_gain": gain(ks[15], (L, CONV_CH)),
        "conv_ln_bias": nrm(ks[16], (L, CONV_CH), 0.02),
        "w_conv_branch": nrm(ks[17], (L, CONV_CH, D), CONV_CH ** -0.5),
        "w_mix_out": nrm(ks[18], (L, D, D), D ** -0.5),
        "ffn2_norm": gain(ks[19], (L, D)),
        "ffn2_w_in": nrm(ks[20], (L, D, 2 * D_FF), D ** -0.5),
        "ffn2_w_out": nrm(ks[21], (L, D_FF, D), D_FF ** -0.5),
        "ada_w": nrm(ks[22], (L, D, N_MOD * D), 0.5 * D ** -0.5),
        "ada_b": nrm(ks[23], (L, N_MOD * D), 0.02),
        "rel_bias": nrm(ks[24], (N_HEADS, NUM_BUCKETS), 0.5),
        "final_norm": gain(ks[25], (D,)),
    }


def reference(x, c, ffn1_norm, ffn1_w_in, ffn1_w_out, mix_norm, w_mix_in,
              lambda_q1, lambda_k1, lambda_q2, lambda_k2, subln_gain,
              w_attn_branch, conv_dw, conv_dw_bias, conv_ln_gain, conv_ln_bias,
              w_conv_branch, w_mix_out, ffn2_norm, ffn2_w_in, ffn2_w_out,
              ada_w, ada_b, rel_bias, final_norm):
    B, S, _ = x.shape
    c_act = jax.nn.silu(c)
    for l in range(DEPTH):
        lam_init = 0.8 - 0.6 * math.exp(-0.3 * l)
        mod = c_act @ ada_w[l] + ada_b[l]
        sh1, sc1, g1, sh2, sc2, g2, sh3, sc3, g3 = jnp.split(mod, N_MOD, axis=-1)

        h = modulate(rms_norm(x, ffn1_norm[l]), sh1, sc1)
        x = x + 0.5 * g1[:, None, :] * swiglu(h, ffn1_w_in[l], ffn1_w_out[l])

        h = modulate(rms_norm(x, mix_norm[l]), sh2, sc2)
        proj = h @ w_mix_in[l]
        q, k, v, u, gates = jnp.split(
            proj, [ATTN_W, 2 * ATTN_W, 3 * ATTN_W, 3 * ATTN_W + 2 * CONV_CH], axis=-1)
        q = q.reshape(B, S, N_HEADS, 2, HEAD_DIM)
        k = k.reshape(B, S, N_HEADS, 2, HEAD_DIM)
        v = v.reshape(B, S, N_HEADS, 2 * HEAD_DIM)
        lam = (jnp.exp(jnp.sum(lambda_q1[l].astype(jnp.float32) * lambda_k1[l].astype(jnp.float32)))
               - jnp.exp(jnp.sum(lambda_q2[l].astype(jnp.float32) * lambda_k2[l].astype(jnp.float32)))
               + lam_init)
        o = diff_attention(q, k, v, lam, rel_bias)
        o = rms_norm(o, subln_gain[l]) * (1 - lam_init)
        y_attn = o.reshape(B, S, ATTN_W) @ w_attn_branch[l]
        y_conv = conformer_conv(u, conv_dw[l], conv_dw_bias[l], conv_ln_gain[l],
                                conv_ln_bias[l], w_conv_branch[l])
        ga, gb = jnp.split(gates, N_BRANCH, axis=-1)
        merged = jax.nn.sigmoid(ga) * y_attn + jax.nn.sigmoid(gb) * y_conv
        x = x + g2[:, None, :] * (merged @ w_mix_out[l])

        h = modulate(rms_norm(x, ffn2_norm[l]), sh3, sc3)
        x = x + 0.5 * g3[:, None, :] * swiglu(h, ffn2_w_in[l], ffn2_w_out[l])
    return rms_norm(x, final_norm)
```

```python
import functools
import math

import jax
import jax.numpy as jnp
from jax import lax
from jax.experimental import pallas as pl
from jax.experimental.pallas import tpu as pltpu

D_MODEL = 1024
DEPTH = 2
N_HEADS = 8
HEAD_DIM = 64
HEAD_W = 2 * HEAD_DIM
ATTN_W = N_HEADS * HEAD_W
CONV_K = 31
D_FF = 2816
NUM_BUCKETS = 32
MAX_DISTANCE = 128
N_MOD = 9
EPS = 1e-6

SEQ_TILE = 512
CONV_HALO = 32
FF_CHUNK = 1408
MASK_BIAS = -1e30
VMEM_LIMIT = 56 * 1024 * 1024

F32 = jnp.float32
BF16 = jnp.bfloat16


def _params(*sem):
    return pltpu.CompilerParams(dimension_semantics=sem, vmem_limit_bytes=VMEM_LIMIT)


def _resident(shape):
    return pl.BlockSpec(shape, lambda *_: (0,) * len(shape), pipeline_mode=pl.Buffered(1))


def _rms_norm(x, g):
    return x * lax.rsqrt(jnp.mean(x * x, axis=-1, keepdims=True) + EPS) * g


def _mod_kernel(c_ref, w_ref, b_ref, o_ref):
    c = c_ref[...]
    c_act = (c * jax.nn.sigmoid(c)).astype(BF16)
    o_ref[...] = jnp.dot(c_act, w_ref[...].astype(BF16), preferred_element_type=F32) + b_ref[...]


def _modulation(c_pad, ada_w, ada_b):
    rows = c_pad.shape[0]
    return pl.pallas_call(
        _mod_kernel,
        grid=(DEPTH, N_MOD),
        in_specs=[
            pl.BlockSpec((rows, D_MODEL), lambda l, j: (0, 0)),
            pl.BlockSpec((None, D_MODEL, D_MODEL), lambda l, j: (l, 0, j)),
            pl.BlockSpec((None, 1, D_MODEL), lambda l, j: (l, 0, j)),
        ],
        out_specs=pl.BlockSpec((None, rows, D_MODEL), lambda l, j: (l, 0, j)),
        out_shape=jax.ShapeDtypeStruct((DEPTH, rows, N_MOD * D_MODEL), F32),
        compiler_params=_params("arbitrary", "arbitrary"),
        name="adaln_mod",
    )(c_pad, ada_w, ada_b.reshape(DEPTH, 1, N_MOD * D_MODEL))


def _ffn_kernel(x_ref, mod_ref, g_ref, w_in_ref, w_out_ref, fg_ref, o_ref, *, mod_row, final):
    x = x_ref[...]
    shift = mod_ref[mod_row:mod_row + 1, :]
    scale = mod_ref[mod_row + 1:mod_row + 2, :]
    gate = mod_ref[mod_row + 2:mod_row + 3, :]
    h = (_rms_norm(x, g_ref[...]) * (1 + scale) + shift).astype(BF16)
    out = None
    for c0 in range(0, D_FF, FF_CHUNK):
        g = jnp.dot(h, w_in_ref[:, c0:c0 + FF_CHUNK], preferred_element_type=F32)
        u = jnp.dot(h, w_in_ref[:, D_FF + c0:D_FF + c0 + FF_CHUNK], preferred_element_type=F32)
        act = (g * jax.nn.sigmoid(g) * u).astype(BF16)
        part = jnp.dot(act, w_out_ref[c0:c0 + FF_CHUNK, :], preferred_element_type=F32)
        out = part if out is None else out + part
    y = x + (0.5 * gate) * out
    if final:
        y = _rms_norm(y, fg_ref[...])
    o_ref[...] = y


def _ffn(x, mod_l, norm_g, w_in, w_out, final_g, *, mod_row, final):
    B, S, _ = x.shape
    tile = pl.BlockSpec((None, SEQ_TILE, D_MODEL), lambda b, i: (b, i, 0))
    return pl.pallas_call(
        functools.partial(_ffn_kernel, mod_row=mod_row, final=final),
        grid=(B, S // SEQ_TILE),
        in_specs=[
            tile,
            pl.BlockSpec((None, N_MOD, D_MODEL), lambda b, i: (b, 0, 0)),
            _resident((1, D_MODEL)),
            _resident((D_MODEL, 2 * D_FF)),
            _resident((D_FF, D_MODEL)),
            _resident((1, D_MODEL)),
        ],
        out_specs=tile,
        out_shape=jax.ShapeDtypeStruct(x.shape, F32),
        compiler_params=_params("parallel", "parallel"),
        name="ffn",
    )(x, mod_l, norm_g.reshape(1, D_MODEL), w_in, w_out, final_g.reshape(1, D_MODEL))


def _mix_in_kernel(x_ref, mod_ref, g_ref, w_qk_ref, w_vt_ref, w_u_ref, w_g_ref,
                   q_ref, k_ref, vt_ref, z_ref, gates_ref):
    x = x_ref[...]
    shift = mod_ref[3:4, :]
    scale = mod_ref[4:5, :]
    h = (_rms_norm(x, g_ref[...]) * (1 + scale) + shift).astype(BF16)
    q = jnp.dot(h, w_qk_ref[:, :ATTN_W], preferred_element_type=F32)
    q_ref[...] = (q * HEAD_DIM ** -0.5).astype(BF16)
    k_ref[...] = jnp.dot(h, w_qk_ref[:, ATTN_W:], preferred_element_type=F32).astype(BF16)
    vt = lax.dot_general(w_vt_ref[...], h, (((1,), (1,)), ((), ())), preferred_element_type=F32)
    vt_ref[...] = vt.astype(BF16).reshape(N_HEADS, HEAD_W, SEQ_TILE)
    a = jnp.dot(h, w_u_ref[:, :D_MODEL], preferred_element_type=F32)
    g = jnp.dot(h, w_u_ref[:, D_MODEL:], preferred_element_type=F32)
    z_ref[...] = a * jax.nn.sigmoid(g)
    gates_ref[...] = jax.nn.sigmoid(jnp.dot(h, w_g_ref[...], preferred_element_type=F32))


def _mix_in(x, mod_l, norm_g, w_qk, w_vt, w_u, w_g):
    B, S, _ = x.shape
    nt = S // SEQ_TILE
    tile = pl.BlockSpec((None, SEQ_TILE, D_MODEL), lambda b, i: (b, i, 0))
    return pl.pallas_call(
        _mix_in_kernel,
        grid=(B, nt),
        in_specs=[
            tile,
            pl.BlockSpec((None, N_MOD, D_MODEL), lambda b, i: (b, 0, 0)),
            _resident((1, D_MODEL)),
            _resident((D_MODEL, 2 * ATTN_W)),
            _resident((ATTN_W, D_MODEL)),
            _resident((D_MODEL, 2 * D_MODEL)),
            _resident((D_MODEL, 2 * D_MODEL)),
        ],
        out_specs=[
            tile,
            tile,
            pl.BlockSpec((None, None, N_HEADS, HEAD_W, SEQ_TILE), lambda b, i: (b, i, 0, 0, 0)),
            tile,
            pl.BlockSpec((None, SEQ_TILE, 2 * D_MODEL), lambda b, i: (b, i, 0)),
        ],
        out_shape=[
            jax.ShapeDtypeStruct((B, S, ATTN_W), BF16),
            jax.ShapeDtypeStruct((B, S, ATTN_W), BF16),
            jax.ShapeDtypeStruct((B, nt, N_HEADS, HEAD_W, SEQ_TILE), BF16),
            jax.ShapeDtypeStruct((B, S, D_MODEL), F32),
            jax.ShapeDtypeStruct((B, S, 2 * D_MODEL), F32),
        ],
        compiler_params=_params("parallel", "parallel"),
        name="mix_in",
    )(x, mod_l, norm_g.reshape(1, D_MODEL), w_qk, w_vt, w_u, w_g)


def _bias_kernel(table_ref, o_ref):
    h = pl.program_id(0)
    t = pl.program_id(1)
    shape = (SEQ_TILE, SEQ_TILE)
    dist = (lax.broadcasted_iota(jnp.int32, shape, 1) - lax.broadcasted_iota(jnp.int32, shape, 0)
            + t * SEQ_TILE)
    n = jnp.maximum(dist, 0)
    max_exact = NUM_BUCKETS // 2
    nf = jnp.maximum(n, 1).astype(F32)
    large = max_exact + (jnp.log(nf / max_exact) / math.log(MAX_DISTANCE / max_exact)
                         * (NUM_BUCKETS - max_exact)).astype(jnp.int32)
    large = jnp.minimum(large, NUM_BUCKETS - 1)
    bucket = jnp.where(n < max_exact, n, large)
    far = table_ref[h, NUM_BUCKETS - 1]
    val = jnp.zeros(shape, F32)
    for b in range(NUM_BUCKETS - 1):
        val = jnp.where(bucket == b, table_ref[h, b] - far, val)
    o_ref[...] = jnp.where(dist >= 0, val, MASK_BIAS)


def _bias_tiles(rel_bias):
    return pl.pallas_call(
        _bias_kernel,
        grid=(N_HEADS, 2),
        in_specs=[pl.BlockSpec(memory_space=pltpu.SMEM)],
        out_specs=pl.BlockSpec((None, None, SEQ_TILE, SEQ_TILE), lambda h, t: (h, t, 0, 0)),
        out_shape=jax.ShapeDtypeStruct((N_HEADS, 2, SEQ_TILE, SEQ_TILE), F32),
        compiler_params=_params("arbitrary", "arbitrary"),
        name="rel_bias_tiles",
    )(rel_bias)


def _attn_kernel(q_ref, k_ref, vt_ref, bias_ref, lamv_ref, sg_ref, o_ref,
                 qs_ref, m_ref, l_ref, acc_ref, *, lam_init):
    T = SEQ_TILE
    qi = pl.program_id(2)

    q = q_ref[...]
    lane = lax.broadcasted_iota(jnp.int32, q.shape, 1)
    zero = jnp.zeros_like(q)
    qs_ref[0:T, :] = jnp.where(lane < HEAD_DIM, q, zero)
    qs_ref[T:2 * T, :] = jnp.where(lane >= HEAD_DIM, q, zero)

    m_ref[...] = jnp.full(m_ref.shape, MASK_BIAS, F32)
    l_ref[...] = jnp.zeros(l_ref.shape, F32)
    acc_ref[...] = jnp.zeros(acc_ref.shape, F32)

    def step(ki, bias):
        kb = k_ref[pl.ds(pl.multiple_of(ki * T, T), T), :]
        s = lax.dot_general(kb, qs_ref[...], (((1,), (1,)), ((), ())),
                            preferred_element_type=F32)
        if bias is not None:
            s = s + jnp.concatenate([bias, bias], axis=1)
        m_prev = m_ref[...]
        m_new = jnp.maximum(m_prev, jnp.max(s, axis=0, keepdims=True))
        alpha = jnp.exp(m_prev - m_new)
        p = jnp.exp(s - m_new)
        l_ref[...] = alpha * l_ref[...] + jnp.sum(p, axis=0, keepdims=True)
        pv = jnp.dot(vt_ref[ki], p.astype(BF16), preferred_element_type=F32)
        acc_ref[...] = alpha * acc_ref[...] + pv
        m_ref[...] = m_new

    step(qi, bias_ref[0])

    @pl.when(qi >= 1)
    def _():
        step(qi - 1, bias_ref[1])

    def far(ki, carry):
        step(ki, None)
        return carry

    lax.fori_loop(0, qi - 1, far, 0)

    lamv = lamv_ref[...]
    lam = (jnp.exp(jnp.sum(lamv[0:1] * lamv[1:2], axis=1, keepdims=True))
           - jnp.exp(jnp.sum(lamv[2:3] * lamv[3:4], axis=1, keepdims=True)) + lam_init)
    o_all = acc_ref[...] / l_ref[...]
    o = o_all[:, :T] - lam * o_all[:, T:]
    o = o * lax.rsqrt(jnp.mean(o * o, axis=0, keepdims=True) + EPS) * sg_ref[...]
    o = o * (1 - lam_init)
    o_ref[...] = o.T.astype(BF16)


def _attention(q, k, vt, bias, lamv, subln_g, *, lam_init):
    B, S, _ = q.shape
    T = SEQ_TILE
    nt = S // T
    return pl.pallas_call(
        functools.partial(_attn_kernel, lam_init=lam_init),
        grid=(B, N_HEADS, nt),
        in_specs=[
            pl.BlockSpec((None, T, HEAD_W), lambda b, h, i: (b, i, h)),
            pl.BlockSpec((None, S, HEAD_W), lambda b, h, i: (b, 0, h)),
            pl.BlockSpec((None, nt, None, HEAD_W, T), lambda b, h, i: (b, 0, h, 0, 0)),
            pl.BlockSpec((None, 2, T, T), lambda b, h, i: (h, 0, 0, 0)),
            pl.BlockSpec((4, HEAD_DIM), lambda b, h, i: (0, 0)),
            pl.BlockSpec((HEAD_W, 1), lambda b, h, i: (0, 0)),
        ],
        out_specs=pl.BlockSpec((None, T, HEAD_W), lambda b, h, i: (b, i, h)),
        out_shape=jax.ShapeDtypeStruct((B, S, ATTN_W), BF16),
        scratch_shapes=[
            pltpu.VMEM((2 * T, HEAD_W), BF16),
            pltpu.VMEM((1, 2 * T), F32),
            pltpu.VMEM((1, 2 * T), F32),
            pltpu.VMEM((HEAD_W, 2 * T), F32),
        ],
        compiler_params=_params("parallel", "parallel", "arbitrary"),
        name="diff_attn",
    )(q, k, vt, bias, lamv, subln_g.reshape(HEAD_W, 1))


def _conv_kernel(z_ref, zprev_ref, w_ref, b_ref, lng_ref, lnb_ref, o_ref, zext_ref):
    T = SEQ_TILE
    i = pl.program_id(1)
    prev = zprev_ref[...]
    zext_ref[0:CONV_HALO, :] = jnp.where(i == 0, jnp.zeros_like(prev), prev)
    zext_ref[CONV_HALO:, :] = z_ref[...]
    rows = 8
    lead = CONV_HALO - (CONV_K - 1)
    for r0 in range(0, T, rows):
        acc = w_ref[0:1, :] * zext_ref[r0 + lead:r0 + lead + rows, :]
        for j in range(1, CONV_K):
            acc = acc + w_ref[j:j + 1, :] * zext_ref[r0 + lead + j:r0 + lead + j + rows, :]
        acc = acc + b_ref[...]
        mu = jnp.mean(acc, axis=-1, keepdims=True)
        xc = acc - mu
        y = xc * lax.rsqrt(jnp.mean(xc * xc, axis=-1, keepdims=True) + EPS)
        y = y * lng_ref[...] + lnb_ref[...]
        o_ref[r0:r0 + rows, :] = (y * jax.nn.sigmoid(y)).astype(BF16)


def _conv_module(z, w_dw, b_dw, ln_g, ln_b):
    B, S, C = z.shape
    T = SEQ_TILE
    halo_blocks = T // CONV_HALO
    vec = lambda a: a.reshape(1, C)
    return pl.pallas_call(
        _conv_kernel,
        grid=(B, S // T),
        in_specs=[
            pl.BlockSpec((None, T, C), lambda b, i: (b, i, 0)),
            pl.BlockSpec((None, CONV_HALO, C), lambda b, i: (b, jnp.maximum(i * halo_blocks - 1, 0), 0)),
            _resident((CONV_K, C)),
            _resident((1, C)),
            _resident((1, C)),
            _resident((1, C)),
        ],
        out_specs=pl.BlockSpec((None, T, C), lambda b, i: (b, i, 0)),
        out_shape=jax.ShapeDtypeStruct((B, S, C), BF16),
        scratch_shapes=[pltpu.VMEM((T + CONV_HALO, C), F32)],
        compiler_params=_params("parallel", "parallel"),
        name="conv_module",
    )(z, z, w_dw, vec(b_dw), vec(ln_g), vec(ln_b))


def _mix_out_kernel(x_ref, mod_ref, o_ref, cact_ref, gates_ref, w_ab_ref, w_cb_ref, w_mo_ref, y_ref):
    y_attn = jnp.dot(o_ref[...], w_ab_ref[...], preferred_element_type=F32)
    y_conv = jnp.dot(cact_ref[...], w_cb_ref[...], preferred_element_type=F32)
    merged = gates_ref[:, :D_MODEL] * y_attn + gates_ref[:, D_MODEL:] * y_conv
    out = jnp.dot(merged.astype(BF16), w_mo_ref[...], preferred_element_type=F32)
    y_ref[...] = x_ref[...] + mod_ref[5:6, :] * out


def _mix_out(x, mod_l, o, cact, gates, w_ab, w_cb, w_mo):
    B, S, _ = x.shape
    tile = pl.BlockSpec((None, SEQ_TILE, D_MODEL), lambda b, i: (b, i, 0))
    return pl.pallas_call(
        _mix_out_kernel,
        grid=(B, S // SEQ_TILE),
        in_specs=[
            tile,
            pl.BlockSpec((None, N_MOD, D_MODEL), lambda b, i: (b, 0, 0)),
            tile,
            tile,
            pl.BlockSpec((None, SEQ_TILE, 2 * D_MODEL), lambda b, i: (b, i, 0)),
            _resident((D_MODEL, D_MODEL)),
            _resident((D_MODEL, D_MODEL)),
            _resident((D_MODEL, D_MODEL)),
        ],
        out_specs=tile,
        out_shape=jax.ShapeDtypeStruct(x.shape, F32),
        compiler_params=_params("parallel", "parallel"),
        name="mix_out",
    )(x, mod_l, o, cact, gates, w_ab, w_cb, w_mo)


def kernel(x, c, ffn1_norm, ffn1_w_in, ffn1_w_out, mix_norm, w_mix_in, lambda_q1, lambda_k1, lambda_q2, lambda_k2, subln_gain, w_attn_branch, conv_dw, conv_dw_bias, conv_ln_gain, conv_ln_bias, w_conv_branch, w_mix_out, ffn2_norm, ffn2_w_in, ffn2_w_out, ada_w, ada_b, rel_bias, final_norm):
    B, S, _ = x.shape
    assert S % SEQ_TILE == 0 and SEQ_TILE % CONV_HALO == 0 and CONV_HALO >= CONV_K - 1
    assert D_FF % FF_CHUNK == 0
    assert SEQ_TILE >= MAX_DISTANCE

    mod_rows = 8
    c_pad = jnp.zeros((mod_rows, D_MODEL), F32).at[:B].set(c)
    mod = _modulation(c_pad, ada_w, ada_b)[:, :B].reshape(DEPTH, B, N_MOD, D_MODEL)
    bias = _bias_tiles(rel_bias)

    bf = lambda w: w.astype(BF16)
    for l in range(DEPTH):
        lam_init = 0.8 - 0.6 * math.exp(-0.3 * l)
        mod_l = mod[l]
        x = _ffn(x, mod_l, ffn1_norm[l], bf(ffn1_w_in[l]), bf(ffn1_w_out[l]), final_norm,
                 mod_row=0, final=False)

        w_mix = w_mix_in[l]
        q, k, vt, z, gates = _mix_in(
            x, mod_l, mix_norm[l],
            bf(w_mix[:, :2 * ATTN_W]),
            bf(w_mix[:, 2 * ATTN_W:3 * ATTN_W].T),
            bf(w_mix[:, 3 * ATTN_W:3 * ATTN_W + 2 * D_MODEL]),
            bf(w_mix[:, 3 * ATTN_W + 2 * D_MODEL:]))
        lamv = jnp.stack([lambda_q1[l], lambda_k1[l], lambda_q2[l], lambda_k2[l]]).astype(F32)
        o = _attention(q, k, vt, bias, lamv, subln_gain[l], lam_init=lam_init)
        cact = _conv_module(z, conv_dw[l], conv_dw_bias[l], conv_ln_gain[l], conv_ln_bias[l])
        x = _mix_out(x, mod_l, o, cact, gates, bf(w_attn_branch[l]), bf(w_conv_branch[l]),
                     bf(w_mix_out[l]))

        x = _ffn(x, mod_l, ffn2_norm[l], bf(ffn2_w_in[l]), bf(ffn2_w_out[l]), final_norm,
                 mod_row=6, final=(l == DEPTH - 1))
    return x
```

```python
import functools
import math

import jax
import jax.numpy as jnp
from jax import lax
from jax.experimental import pallas as pl
from jax.experimental.pallas import tpu as pltpu

D_MODEL = 1024
DEPTH = 2
N_HEADS = 8
HEAD_DIM = 64
HEAD_W = 2 * HEAD_DIM
ATTN_W = N_HEADS * HEAD_W
CONV_K = 31
D_FF = 2816
NUM_BUCKETS = 32
MAX_DISTANCE = 128
N_MOD = 9
EPS = 1e-6

SUBLANES = 8
LANES = 128

SEQ_TILE = 512
KV_TILE = 256
N_BIAS_TILES = 4
LOG2E = math.log2(math.e)
CONV_HALO = 32
FF_CHUNK = 1408
MASK_BIAS = -1e30
VMEM_LIMIT = 56 * 1024 * 1024

F32 = jnp.float32
BF16 = jnp.bfloat16


def _params(*sem):
    return pltpu.CompilerParams(dimension_semantics=sem, vmem_limit_bytes=VMEM_LIMIT)


def _resident(shape):
    return pl.BlockSpec(shape, lambda *_: (0,) * len(shape), pipeline_mode=pl.Buffered(1))


def _rms_norm(x, g):
    return x * lax.rsqrt(jnp.mean(x * x, axis=-1, keepdims=True) + EPS) * g


def _mod_kernel(c_ref, w_ref, b_ref, o_ref):
    c = c_ref[...]
    c_act = (c * jax.nn.sigmoid(c)).astype(BF16)
    o_ref[...] = jnp.dot(c_act, w_ref[...].astype(BF16), preferred_element_type=F32) + b_ref[...]


def _modulation(c_pad, ada_w, ada_b):
    rows = c_pad.shape[0]
    return pl.pallas_call(
        _mod_kernel,
        grid=(DEPTH, N_MOD),
        in_specs=[
            pl.BlockSpec((rows, D_MODEL), lambda l, j: (0, 0)),
            pl.BlockSpec((None, D_MODEL, D_MODEL), lambda l, j: (l, 0, j)),
            pl.BlockSpec((None, 1, D_MODEL), lambda l, j: (l, 0, j)),
        ],
        out_specs=pl.BlockSpec((None, rows, D_MODEL), lambda l, j: (l, 0, j)),
        out_shape=jax.ShapeDtypeStruct((DEPTH, rows, N_MOD * D_MODEL), F32),
        compiler_params=_params("arbitrary", "arbitrary"),
        name="adaln_mod",
    )(c_pad, ada_w, ada_b.reshape(DEPTH, 1, N_MOD * D_MODEL))


def _ffn_kernel(x_ref, mod_ref, g_ref, w_in_ref, w_out_ref, fg_ref, o_ref, *, mod_row, final):
    x = x_ref[...]
    shift = mod_ref[mod_row:mod_row + 1, :]
    scale = mod_ref[mod_row + 1:mod_row + 2, :]
    gate = mod_ref[mod_row + 2:mod_row + 3, :]
    h = (_rms_norm(x, g_ref[...]) * (1 + scale) + shift).astype(BF16)
    out = None
    for c0 in range(0, D_FF, FF_CHUNK):
        g = jnp.dot(h, w_in_ref[:, c0:c0 + FF_CHUNK], preferred_element_type=F32)
        u = jnp.dot(h, w_in_ref[:, D_FF + c0:D_FF + c0 + FF_CHUNK], preferred_element_type=F32)
        act = (g * jax.nn.sigmoid(g) * u).astype(BF16)
        part = jnp.dot(act, w_out_ref[c0:c0 + FF_CHUNK, :], preferred_element_type=F32)
        out = part if out is None else out + part
    y = x + (0.5 * gate) * out
    if final:
        y = _rms_norm(y, fg_ref[...])
    o_ref[...] = y


def _ffn(x, mod_l, norm_g, w_in, w_out, final_g, *, mod_row, final):
    B, S, _ = x.shape
    tile = pl.BlockSpec((None, SEQ_TILE, D_MODEL), lambda b, i: (b, i, 0))
    return pl.pallas_call(
        functools.partial(_ffn_kernel, mod_row=mod_row, final=final),
        grid=(B, S // SEQ_TILE),
        in_specs=[
            tile,
            pl.BlockSpec((None, N_MOD, D_MODEL), lambda b, i: (b, 0, 0)),
            _resident((1, D_MODEL)),
            _resident((D_MODEL, 2 * D_FF)),
            _resident((D_FF, D_MODEL)),
            _resident((1, D_MODEL)),
        ],
        out_specs=tile,
        out_shape=jax.ShapeDtypeStruct(x.shape, F32),
        compiler_params=_params("parallel", "parallel"),
        name="ffn",
    )(x, mod_l, norm_g.reshape(1, D_MODEL), w_in, w_out, final_g.reshape(1, D_MODEL))


def _mix_in_kernel(x_ref, mod_ref, g_ref, w_qk_ref, w_vt_ref, w_u_ref, w_g_ref,
                   q_ref, k_ref, vt_ref, z_ref, gates_ref):
    x = x_ref[...]
    shift = mod_ref[3:4, :]
    scale = mod_ref[4:5, :]
    h = (_rms_norm(x, g_ref[...]) * (1 + scale) + shift).astype(BF16)
    q = jnp.dot(h, w_qk_ref[:, :ATTN_W], preferred_element_type=F32)
    q_ref[...] = (q * (HEAD_DIM ** -0.5 * LOG2E)).astype(BF16)
    k_ref[...] = jnp.dot(h, w_qk_ref[:, ATTN_W:], preferred_element_type=F32).astype(BF16)
    vt = lax.dot_general(w_vt_ref[...], h, (((1,), (1,)), ((), ())), preferred_element_type=F32)
    vt_ref[...] = vt.astype(BF16).reshape(N_HEADS, HEAD_W, SEQ_TILE)
    a = jnp.dot(h, w_u_ref[:, :D_MODEL], preferred_element_type=F32)
    g = jnp.dot(h, w_u_ref[:, D_MODEL:], preferred_element_type=F32)
    z_ref[...] = a * jax.nn.sigmoid(g)
    gates_ref[...] = jax.nn.sigmoid(jnp.dot(h, w_g_ref[...], preferred_element_type=F32))


def _mix_in(x, mod_l, norm_g, w_qk, w_vt, w_u, w_g):
    B, S, _ = x.shape
    nt = S // SEQ_TILE
    tile = pl.BlockSpec((None, SEQ_TILE, D_MODEL), lambda b, i: (b, i, 0))
    return pl.pallas_call(
        _mix_in_kernel,
        grid=(B, nt),
        in_specs=[
            tile,
            pl.BlockSpec((None, N_MOD, D_MODEL), lambda b, i: (b, 0, 0)),
            _resident((1, D_MODEL)),
            _resident((D_MODEL, 2 * ATTN_W)),
            _resident((ATTN_W, D_MODEL)),
            _resident((D_MODEL, 2 * D_MODEL)),
            _resident((D_MODEL, 2 * D_MODEL)),
        ],
        out_specs=[
            tile,
            tile,
            pl.BlockSpec((None, None, N_HEADS, HEAD_W, SEQ_TILE), lambda b, i: (b, i, 0, 0, 0)),
            tile,
            pl.BlockSpec((None, SEQ_TILE, 2 * D_MODEL), lambda b, i: (b, i, 0)),
        ],
        out_shape=[
            jax.ShapeDtypeStruct((B, S, ATTN_W), BF16),
            jax.ShapeDtypeStruct((B, S, ATTN_W), BF16),
            jax.ShapeDtypeStruct((B, nt, N_HEADS, HEAD_W, SEQ_TILE), BF16),
            jax.ShapeDtypeStruct((B, S, D_MODEL), F32),
            jax.ShapeDtypeStruct((B, S, 2 * D_MODEL), F32),
        ],
        compiler_params=_params("parallel", "parallel"),
        name="mix_in",
    )(x, mod_l, norm_g.reshape(1, D_MODEL), w_qk, w_vt, w_u, w_g)


def _bias_kernel(table_ref, o_ref):
    h = pl.program_id(0)
    t = pl.program_id(1)
    shape = (KV_TILE, SEQ_TILE)
    dist = (lax.broadcasted_iota(jnp.int32, shape, 1) - lax.broadcasted_iota(jnp.int32, shape, 0)
            + (t - 1) * KV_TILE)
    n = jnp.maximum(dist, 0)
    max_exact = NUM_BUCKETS // 2
    nf = jnp.maximum(n, 1).astype(F32)
    large = max_exact + (jnp.log(nf / max_exact) / math.log(MAX_DISTANCE / max_exact)
                         * (NUM_BUCKETS - max_exact)).astype(jnp.int32)
    large = jnp.minimum(large, NUM_BUCKETS - 1)
    bucket = jnp.where(n < max_exact, n, large)
    far = table_ref[h, NUM_BUCKETS - 1]
    val = jnp.zeros(shape, F32)
    for b in range(NUM_BUCKETS - 1):
        val = jnp.where(bucket == b, (table_ref[h, b] - far) * LOG2E, val)
    o_ref[...] = jnp.where(dist >= 0, val, MASK_BIAS)


def _bias_tiles(rel_bias):
    return pl.pallas_call(
        _bias_kernel,
        grid=(N_HEADS, N_BIAS_TILES),
        in_specs=[pl.BlockSpec(memory_space=pltpu.SMEM)],
        out_specs=pl.BlockSpec((None, None, KV_TILE, SEQ_TILE), lambda h, t: (h, t, 0, 0)),
        out_shape=jax.ShapeDtypeStruct((N_HEADS, N_BIAS_TILES, KV_TILE, SEQ_TILE), F32),
        compiler_params=_params("arbitrary", "arbitrary"),
        name="rel_bias_tiles",
    )(rel_bias)


def _attn_kernel(q_ref, k_ref, vt_ref, bias_ref, lamv_ref, sg_ref, o_ref,
                 qs_ref, s0_ref, s1_ref, mb0_ref, mb1_ref, p0_ref, p1_ref, al0_ref, al1_ref,
                 m_ref, l_ref, acc_ref, *, lam_init):
    T = SEQ_TILE
    TK = KV_TILE
    qi = pl.program_id(2)
    last_kb = (T // TK) * qi + (T // TK - 1)

    q = q_ref[...]
    lane = lax.broadcasted_iota(jnp.int32, q.shape, 1)
    zero = jnp.zeros_like(q)
    qs_ref[0:T, :] = jnp.where(lane < HEAD_DIM, q, zero)
    qs_ref[T:2 * T, :] = jnp.where(lane >= HEAD_DIM, q, zero)

    m_ref[...] = jnp.full(m_ref.shape, MASK_BIAS, F32)
    l_ref[...] = jnp.zeros(l_ref.shape, F32)
    acc_ref[...] = jnp.zeros(acc_ref.shape, F32)

    def scores(kb, s_ref, mb_ref):
        kblk = k_ref[pl.ds(pl.multiple_of(kb * TK, TK), TK), :]
        s = lax.dot_general(kblk, qs_ref[...], (((1,), (1,)), ((), ())),
                            preferred_element_type=F32)
        bias = bias_ref[jnp.clip(last_kb - kb, 0, N_BIAS_TILES - 1)]
        s = jnp.concatenate([s[:, :T] + bias, s[:, T:] + bias], axis=1)
        s_ref[...] = s
        mb_ref[...] = jnp.max(s, axis=0, keepdims=True)

    def softmax(s_ref, mb_ref, p_ref, al_ref):
        m_prev = m_ref[...]
        m_new = jnp.maximum(m_prev, mb_ref[...])
        alpha = jnp.exp2(m_prev - m_new)
        p = jnp.exp2(s_ref[...] - m_new)
        l_ref[...] = alpha * l_ref[...] + jnp.sum(p, axis=0, keepdims=True)
        p_ref[...] = p.astype(BF16)
        al_ref[...] = alpha
        m_ref[...] = m_new

    def weighted_values(j, half, p_ref, al_ref):
        vt = vt_ref[j, :, half * TK:(half + 1) * TK]
        pv = jnp.dot(vt, p_ref[...], preferred_element_type=F32)
        acc_ref[...] = al_ref[...] * acc_ref[...] + pv

    scores(0, s0_ref, mb0_ref)

    def body(j, carry):
        kb0 = (T // TK) * j
        scores(kb0 + 1, s1_ref, mb1_ref)
        softmax(s0_ref, mb0_ref, p0_ref, al0_ref)
        weighted_values(j, 0, p0_ref, al0_ref)
        scores(jnp.minimum(kb0 + 2, last_kb), s0_ref, mb0_ref)
        softmax(s1_ref, mb1_ref, p1_ref, al1_ref)
        weighted_values(j, 1, p1_ref, al1_ref)
        return carry

    lax.fori_loop(0, qi + 1, body, 0)

    lamv = lamv_ref[...]
    lam = (jnp.exp(jnp.sum(lamv[0:1] * lamv[1:2], axis=1, keepdims=True))
           - jnp.exp(jnp.sum(lamv[2:3] * lamv[3:4], axis=1, keepdims=True)) + lam_init)
    o_all = acc_ref[...] / l_ref[...]
    o = o_all[:, :T] - lam * o_all[:, T:]
    o = o * lax.rsqrt(jnp.mean(o * o, axis=0, keepdims=True) + EPS) * sg_ref[...]
    o = o * (1 - lam_init)
    o_ref[...] = o.T.astype(BF16)


def _attention(q, k, vt, bias, lamv, subln_g, *, lam_init):
    B, S, _ = q.shape
    T = SEQ_TILE
    nt = S // T
    return pl.pallas_call(
        functools.partial(_attn_kernel, lam_init=lam_init),
        grid=(B, N_HEADS, nt),
        in_specs=[
            pl.BlockSpec((None, T, HEAD_W), lambda b, h, i: (b, i, h)),
            pl.BlockSpec((None, S, HEAD_W), lambda b, h, i: (b, 0, h)),
            pl.BlockSpec((None, nt, None, HEAD_W, T), lambda b, h, i: (b, 0, h, 0, 0)),
            pl.BlockSpec((None, N_BIAS_TILES, KV_TILE, T), lambda b, h, i: (h, 0, 0, 0)),
            pl.BlockSpec((4, HEAD_DIM), lambda b, h, i: (0, 0)),
            pl.BlockSpec((HEAD_W, 1), lambda b, h, i: (0, 0)),
        ],
        out_specs=pl.BlockSpec((None, T, HEAD_W), lambda b, h, i: (b, i, h)),
        out_shape=jax.ShapeDtypeStruct((B, S, ATTN_W), BF16),
        scratch_shapes=[
            pltpu.VMEM((2 * T, HEAD_W), BF16),
            pltpu.VMEM((KV_TILE, 2 * T), F32),
            pltpu.VMEM((KV_TILE, 2 * T), F32),
            pltpu.VMEM((1, 2 * T), F32),
            pltpu.VMEM((1, 2 * T), F32),
            pltpu.VMEM((KV_TILE, 2 * T), BF16),
            pltpu.VMEM((KV_TILE, 2 * T), BF16),
            pltpu.VMEM((1, 2 * T), F32),
            pltpu.VMEM((1, 2 * T), F32),
            pltpu.VMEM((1, 2 * T), F32),
            pltpu.VMEM((1, 2 * T), F32),
            pltpu.VMEM((HEAD_W, 2 * T), F32),
        ],
        compiler_params=_params("parallel", "parallel", "arbitrary"),
        name="diff_attn",
    )(q, k, vt, bias, lamv, subln_g.reshape(HEAD_W, 1))


def _conv_kernel(z_ref, zprev_ref, w8_ref, b_ref, lng_ref, lnb_ref, o_ref, zext_ref, y_ref):
    T = SEQ_TILE
    i = pl.program_id(1)
    prev = zprev_ref[...]
    zext_ref[0:CONV_HALO, :] = jnp.where(i == 0, jnp.zeros_like(prev), prev)
    zext_ref[CONV_HALO:, :] = z_ref[...]

    halo_groups = CONV_HALO // SUBLANES
    n_a = (CONV_K - 1) // SUBLANES + 1
    sub = lax.broadcasted_iota(jnp.int32, (SUBLANES, LANES), 0)

    for c0 in range(0, z_ref.shape[1], LANES):
        cs = slice(c0, c0 + LANES)
        w = [w8_ref[SUBLANES * j:SUBLANES * (j + 1), cs] for j in range(CONV_K)]
        bias = b_ref[:, cs]

        def partial_sums(zs):
            ys = []
            for b in range(SUBLANES):
                acc = None
                for a in range(n_a):
                    s = SUBLANES * a + b
                    if s < CONV_K:
                        t = w[CONV_K - 1 - s] * zs[a]
                        acc = t if acc is None else acc + t
                ys.append(acc)
            return ys

        def group(g):
            return zext_ref[pl.ds(pl.multiple_of(g * SUBLANES, SUBLANES), SUBLANES), cs]

        zs0 = [zext_ref[(halo_groups - 1 - a) * SUBLANES:(halo_groups - a) * SUBLANES, cs]
               for a in range(n_a)]
        ys0 = partial_sums(zs0)
        rolled0 = tuple(pltpu.roll(ys0[b], b, 0) for b in range(1, SUBLANES))

        def body(g, carry):
            rolled_prev, z_hist = carry
            zs = (group(g + halo_groups),) + z_hist
            ys = partial_sums(zs)
            out = ys[0] + bias
            rolled = []
            for b in range(1, SUBLANES):
                r = pltpu.roll(ys[b], b, 0)
                out = out + jnp.where(sub >= b, r, rolled_prev[b - 1])
                rolled.append(r)
            y_ref[pl.ds(pl.multiple_of(g * SUBLANES, SUBLANES), SUBLANES), cs] = out
            return tuple(rolled), zs[:n_a - 1]

        lax.fori_loop(0, T // SUBLANES, body, (rolled0, tuple(zs0[:n_a - 1])), unroll=2)

    rows = 32

    def ln_body(r, carry):
        rs = pl.ds(pl.multiple_of(r * rows, rows), rows)
        acc = y_ref[rs, :]
        mu = jnp.mean(acc, axis=-1, keepdims=True)
        xc = acc - mu
        y = xc * lax.rsqrt(jnp.mean(xc * xc, axis=-1, keepdims=True) + EPS)
        y = y * lng_ref[...] + lnb_ref[...]
        o_ref[rs, :] = (y * jax.nn.sigmoid(y)).astype(BF16)
        return carry

    lax.fori_loop(0, T // rows, ln_body, 0)


def _conv_module(z, w_dw, b_dw, ln_g, ln_b):
    B, S, C = z.shape
    T = SEQ_TILE
    halo_blocks = T // CONV_HALO
    vec = lambda a: a.reshape(1, C)
    w8 = jnp.repeat(w_dw, SUBLANES, axis=0)
    return pl.pallas_call(
        _conv_kernel,
        grid=(B, S // T),
        in_specs=[
            pl.BlockSpec((None, T, C), lambda b, i: (b, i, 0)),
            pl.BlockSpec((None, CONV_HALO, C), lambda b, i: (b, jnp.maximum(i * halo_blocks - 1, 0), 0)),
            _resident((CONV_K * SUBLANES, C)),
            _resident((1, C)),
            _resident((1, C)),
            _resident((1, C)),
        ],
        out_specs=pl.BlockSpec((None, T, C), lambda b, i: (b, i, 0)),
        out_shape=jax.ShapeDtypeStruct((B, S, C), BF16),
        scratch_shapes=[pltpu.VMEM((T + CONV_HALO, C), F32), pltpu.VMEM((T, C), F32)],
        compiler_params=_params("parallel", "parallel"),
        name="conv_module",
    )(z, z, w8, vec(b_dw), vec(ln_g), vec(ln_b))


def _mix_out_kernel(x_ref, mod_ref, o_ref, cact_ref, gates_ref, w_ab_ref, w_cb_ref, w_mo_ref, y_ref):
    y_attn = jnp.dot(o_ref[...], w_ab_ref[...], preferred_element_type=F32)
    y_conv = jnp.dot(cact_ref[...], w_cb_ref[...], preferred_element_type=F32)
    merged = gates_ref[:, :D_MODEL] * y_attn + gates_ref[:, D_MODEL:] * y_conv
    out = jnp.dot(merged.astype(BF16), w_mo_ref[...], preferred_element_type=F32)
    y_ref[...] = x_ref[...] + mod_ref[5:6, :] * out


def _mix_out(x, mod_l, o, cact, gates, w_ab, w_cb, w_mo):
    B, S, _ = x.shape
    tile = pl.BlockSpec((None, SEQ_TILE, D_MODEL), lambda b, i: (b, i, 0))
    return pl.pallas_call(
        _mix_out_kernel,
        grid=(B, S // SEQ_TILE),
        in_specs=[
            tile,
            pl.BlockSpec((None, N_MOD, D_MODEL), lambda b, i: (b, 0, 0)),
            tile,
            tile,
            pl.BlockSpec((None, SEQ_TILE, 2 * D_MODEL), lambda b, i: (b, i, 0)),
            _resident((D_MODEL, D_MODEL)),
            _resident((D_MODEL, D_MODEL)),
            _resident((D_MODEL, D_MODEL)),
        ],
        out_specs=tile,
        out_shape=jax.ShapeDtypeStruct(x.shape, F32),
        compiler_params=_params("parallel", "parallel"),
        name="mix_out",
    )(x, mod_l, o, cact, gates, w_ab, w_cb, w_mo)


def kernel(x, c, ffn1_norm, ffn1_w_in, ffn1_w_out, mix_norm, w_mix_in, lambda_q1, lambda_k1, lambda_q2, lambda_k2, subln_gain, w_attn_branch, conv_dw, conv_dw_bias, conv_ln_gain, conv_ln_bias, w_conv_branch, w_mix_out, ffn2_norm, ffn2_w_in, ffn2_w_out, ada_w, ada_b, rel_bias, final_norm):
    B, S, _ = x.shape
    assert S % SEQ_TILE == 0 and SEQ_TILE % CONV_HALO == 0 and CONV_HALO >= CONV_K - 1
    assert D_FF % FF_CHUNK == 0
    assert SEQ_TILE == 2 * KV_TILE and KV_TILE >= MAX_DISTANCE

    mod_rows = 8
    c_pad = jnp.zeros((mod_rows, D_MODEL), F32).at[:B].set(c)
    mod = _modulation(c_pad, ada_w, ada_b)[:, :B].reshape(DEPTH, B, N_MOD, D_MODEL)
    bias = _bias_tiles(rel_bias)

    bf = lambda w: w.astype(BF16)
    for l in range(DEPTH):
        lam_init = 0.8 - 0.6 * math.exp(-0.3 * l)
        mod_l = mod[l]
        x = _ffn(x, mod_l, ffn1_norm[l], bf(ffn1_w_in[l]), bf(ffn1_w_out[l]), final_norm,
                 mod_row=0, final=False)

        w_mix = w_mix_in[l]
        q, k, vt, z, gates = _mix_in(
            x, mod_l, mix_norm[l],
            bf(w_mix[:, :2 * ATTN_W]),
            bf(w_mix[:, 2 * ATTN_W:3 * ATTN_W].T),
            bf(w_mix[:, 3 * ATTN_W:3 * ATTN_W + 2 * D_MODEL]),
            bf(w_mix[:, 3 * ATTN_W + 2 * D_MODEL:]))
        lamv = jnp.stack([lambda_q1[l], lambda_k1[l], lambda_q2[l], lambda_k2[l]]).astype(F32)
        o = _attention(q, k, vt, bias, lamv, subln_gain[l], lam_init=lam_init)
        cact = _conv_module(z, conv_dw[l], conv_dw_bias[l], conv_ln_gain[l], conv_ln_bias[l])
        x = _mix_out(x, mod_l, o, cact, gates, bf(w_attn_branch[l]), bf(w_conv_branch[l]),
                     bf(w_mix_out[l]))

        x = _ffn(x, mod_l, ffn2_norm[l], bf(ffn2_w_in[l]), bf(ffn2_w_out[l]), final_norm,
                 mod_row=6, final=(l == DEPTH - 1))
    return x
```

```python
import functools
import math

import jax
import jax.numpy as jnp
from jax import lax
from jax.experimental import pallas as pl
from jax.experimental.pallas import tpu as pltpu

D_MODEL = 1024
DEPTH = 2
N_HEADS = 8
HEAD_DIM = 64
HEAD_W = 2 * HEAD_DIM
ATTN_W = N_HEADS * HEAD_W
CONV_K = 31
D_FF = 2816
NUM_BUCKETS = 32
MAX_DISTANCE = 128
N_MOD = 9
EPS = 1e-6

SUBLANES = 8
LANES = 128
MXU_COLS = 256

SEQ_TILE = 512
KV_TILE = 256
FAR_UNROLL = 2
N_BIAS_TILES = 4
LOG2E = math.log2(math.e)
BF16_SUBLANES = 16
VT_ROWS = HEAD_W + BF16_SUBLANES
CONV_HALO = 32
FF_CHUNK = 1408
MASK_BIAS = -1e30
VMEM_LIMIT = 56 * 1024 * 1024

F32 = jnp.float32
BF16 = jnp.bfloat16


def _params(*sem):
    return pltpu.CompilerParams(dimension_semantics=sem, vmem_limit_bytes=VMEM_LIMIT)


def _resident(shape):
    return pl.BlockSpec(shape, lambda *_: (0,) * len(shape), pipeline_mode=pl.Buffered(1))


def _rms_norm(x, g):
    return x * lax.rsqrt(jnp.mean(x * x, axis=-1, keepdims=True) + EPS) * g


def _mod_kernel(c_ref, w_ref, b_ref, o_ref):
    c = c_ref[...]
    c_act = (c * jax.nn.sigmoid(c)).astype(BF16)
    o_ref[...] = jnp.dot(c_act, w_ref[...].astype(BF16), preferred_element_type=F32) + b_ref[...]


def _modulation(c_pad, ada_w, ada_b):
    rows = c_pad.shape[0]
    return pl.pallas_call(
        _mod_kernel,
        grid=(DEPTH, N_MOD),
        in_specs=[
            pl.BlockSpec((rows, D_MODEL), lambda l, j: (0, 0)),
            pl.BlockSpec((None, D_MODEL, D_MODEL), lambda l, j: (l, 0, j)),
            pl.BlockSpec((None, 1, D_MODEL), lambda l, j: (l, 0, j)),
        ],
        out_specs=pl.BlockSpec((None, rows, D_MODEL), lambda l, j: (l, 0, j)),
        out_shape=jax.ShapeDtypeStruct((DEPTH, rows, N_MOD * D_MODEL), F32),
        compiler_params=_params("arbitrary", "arbitrary"),
        name="adaln_mod",
    )(c_pad, ada_w, ada_b.reshape(DEPTH, 1, N_MOD * D_MODEL))


def _ffn_kernel(x_ref, mod_ref, g_ref, w_in_ref, w_out_ref, fg_ref, o_ref, *, mod_row, final):
    x = x_ref[...]
    shift = mod_ref[mod_row:mod_row + 1, :]
    scale = mod_ref[mod_row + 1:mod_row + 2, :]
    gate = mod_ref[mod_row + 2:mod_row + 3, :]
    h = (_rms_norm(x, g_ref[...]) * (1 + scale) + shift).astype(BF16)
    out = None
    for c0 in range(0, D_FF, FF_CHUNK):
        g = jnp.dot(h, w_in_ref[:, c0:c0 + FF_CHUNK], preferred_element_type=F32)
        u = jnp.dot(h, w_in_ref[:, D_FF + c0:D_FF + c0 + FF_CHUNK], preferred_element_type=F32)
        act = (g * jax.nn.sigmoid(g) * u).astype(BF16)
        part = jnp.dot(act, w_out_ref[c0:c0 + FF_CHUNK, :], preferred_element_type=F32)
        out = part if out is None else out + part
    y = x + (0.5 * gate) * out
    if final:
        y = _rms_norm(y, fg_ref[...])
    o_ref[...] = y


def _ffn(x, mod_l, norm_g, w_in, w_out, final_g, *, mod_row, final):
    B, S, _ = x.shape
    tile = pl.BlockSpec((None, SEQ_TILE, D_MODEL), lambda b, i: (b, i, 0))
    return pl.pallas_call(
        functools.partial(_ffn_kernel, mod_row=mod_row, final=final),
        grid=(B, S // SEQ_TILE),
        in_specs=[
            tile,
            pl.BlockSpec((None, N_MOD, D_MODEL), lambda b, i: (b, 0, 0)),
            _resident((1, D_MODEL)),
            _resident((D_MODEL, 2 * D_FF)),
            _resident((D_FF, D_MODEL)),
            _resident((1, D_MODEL)),
        ],
        out_specs=tile,
        out_shape=jax.ShapeDtypeStruct(x.shape, F32),
        compiler_params=_params("parallel", "parallel"),
        name="ffn",
    )(x, mod_l, norm_g.reshape(1, D_MODEL), w_in, w_out, final_g.reshape(1, D_MODEL))


def _mix_in_kernel(x_ref, mod_ref, g_ref, w_qk_ref, w_vt_ref, w_u_ref, w_g_ref,
                   q_ref, k_ref, vt_ref, z_ref, gates_ref):
    x = x_ref[...]
    shift = mod_ref[3:4, :]
    scale = mod_ref[4:5, :]
    h = (_rms_norm(x, g_ref[...]) * (1 + scale) + shift).astype(BF16)
    q = jnp.dot(h, w_qk_ref[:, :ATTN_W], preferred_element_type=F32)
    q_ref[...] = (q * (HEAD_DIM ** -0.5 * LOG2E)).astype(BF16)
    k_ref[...] = jnp.dot(h, w_qk_ref[:, ATTN_W:], preferred_element_type=F32).astype(BF16)
    vt = lax.dot_general(w_vt_ref[...], h, (((1,), (1,)), ((), ())), preferred_element_type=F32)
    vt_ref[:, :HEAD_W, :] = vt.astype(BF16).reshape(N_HEADS, HEAD_W, SEQ_TILE)
    vt_ref[:, HEAD_W:, :] = jnp.ones((N_HEADS, VT_ROWS - HEAD_W, SEQ_TILE), BF16)
    a = jnp.dot(h, w_u_ref[:, :D_MODEL], preferred_element_type=F32)
    g = jnp.dot(h, w_u_ref[:, D_MODEL:], preferred_element_type=F32)
    z_ref[...] = a * jax.nn.sigmoid(g)
    gates_ref[...] = jax.nn.sigmoid(jnp.dot(h, w_g_ref[...], preferred_element_type=F32))


def _mix_in(x, mod_l, norm_g, w_qk, w_vt, w_u, w_g):
    B, S, _ = x.shape
    nt = S // SEQ_TILE
    tile = pl.BlockSpec((None, SEQ_TILE, D_MODEL), lambda b, i: (b, i, 0))
    return pl.pallas_call(
        _mix_in_kernel,
        grid=(B, nt),
        in_specs=[
            tile,
            pl.BlockSpec((None, N_MOD, D_MODEL), lambda b, i: (b, 0, 0)),
            _resident((1, D_MODEL)),
            _resident((D_MODEL, 2 * ATTN_W)),
            _resident((ATTN_W, D_MODEL)),
            _resident((D_MODEL, 2 * D_MODEL)),
            _resident((D_MODEL, 2 * D_MODEL)),
        ],
        out_specs=[
            tile,
            tile,
            pl.BlockSpec((None, None, N_HEADS, VT_ROWS, SEQ_TILE), lambda b, i: (b, i, 0, 0, 0)),
            tile,
            pl.BlockSpec((None, SEQ_TILE, 2 * D_MODEL), lambda b, i: (b, i, 0)),
        ],
        out_shape=[
            jax.ShapeDtypeStruct((B, S, ATTN_W), BF16),
            jax.ShapeDtypeStruct((B, S, ATTN_W), BF16),
            jax.ShapeDtypeStruct((B, nt, N_HEADS, VT_ROWS, SEQ_TILE), BF16),
            jax.ShapeDtypeStruct((B, S, D_MODEL), F32),
            jax.ShapeDtypeStruct((B, S, 2 * D_MODEL), F32),
        ],
        compiler_params=_params("parallel", "parallel"),
        name="mix_in",
    )(x, mod_l, norm_g.reshape(1, D_MODEL), w_qk, w_vt, w_u, w_g)


def _bias_kernel(table_ref, o_ref):
    h = pl.program_id(0)
    t = pl.program_id(1)
    shape = (KV_TILE, SEQ_TILE)
    dist = (lax.broadcasted_iota(jnp.int32, shape, 1) - lax.broadcasted_iota(jnp.int32, shape, 0)
            + (t - 1) * KV_TILE)
    n = jnp.maximum(dist, 0)
    max_exact = NUM_BUCKETS // 2
    nf = jnp.maximum(n, 1).astype(F32)
    large = max_exact + (jnp.log(nf / max_exact) / math.log(MAX_DISTANCE / max_exact)
                         * (NUM_BUCKETS - max_exact)).astype(jnp.int32)
    large = jnp.minimum(large, NUM_BUCKETS - 1)
    bucket = jnp.where(n < max_exact, n, large)
    far = table_ref[h, NUM_BUCKETS - 1]
    val = jnp.zeros(shape, F32)
    for b in range(NUM_BUCKETS - 1):
        val = jnp.where(bucket == b, (table_ref[h, b] - far) * LOG2E, val)
    o_ref[...] = jnp.where(dist >= 0, val, MASK_BIAS)


def _bias_tiles(rel_bias):
    return pl.pallas_call(
        _bias_kernel,
        grid=(N_HEADS, N_BIAS_TILES),
        in_specs=[pl.BlockSpec(memory_space=pltpu.SMEM)],
        out_specs=pl.BlockSpec((None, None, KV_TILE, SEQ_TILE), lambda h, t: (h, t, 0, 0)),
        out_shape=jax.ShapeDtypeStruct((N_HEADS, N_BIAS_TILES, KV_TILE, SEQ_TILE), F32),
        compiler_params=_params("arbitrary", "arbitrary"),
        name="rel_bias_tiles",
    )(rel_bias)


def _attn_kernel(q_ref, k_ref, vt_ref, bias_ref, lamv_ref, sg_ref, o_ref,
                 qs_ref, s0_ref, s1_ref, mb0_ref, mb1_ref, p0_ref, p1_ref, al0_ref, al1_ref,
                 m_ref, acc_ref, *, lam_init):
    T = SEQ_TILE
    TK = KV_TILE
    per_tile = T // TK
    qi = pl.program_id(2)
    last_kb = per_tile * qi + (per_tile - 1)

    q = q_ref[...]
    lane = lax.broadcasted_iota(jnp.int32, q.shape, 1)
    zero = jnp.zeros_like(q)
    qs_ref[0:T, :] = jnp.where(lane < HEAD_DIM, q, zero)
    qs_ref[T:2 * T, :] = jnp.where(lane >= HEAD_DIM, q, zero)

    m_ref[...] = jnp.full(m_ref.shape, MASK_BIAS, F32)
    acc_ref[...] = jnp.zeros(acc_ref.shape, F32)

    col_blocks = [slice(c, c + MXU_COLS) for c in range(0, 2 * T, MXU_COLS)]

    def scores(kb, cs, s_ref, mb_ref):
        kblk = k_ref[pl.ds(pl.multiple_of(kb * TK, TK), TK), :]
        s = lax.dot_general(kblk, qs_ref[cs, :], (((1,), (1,)), ((), ())),
                            preferred_element_type=F32)
        mx = None
        for r in range(0, TK, SUBLANES):
            chunk = s[r:r + SUBLANES, :]
            s_ref[r:r + SUBLANES, cs] = chunk
            mx = chunk if mx is None else jnp.maximum(mx, chunk)
        mb_ref[:, cs] = jnp.max(mx, axis=0, keepdims=True)

    def softmax(kb, cs, s_ref, mb_ref, p_ref, al_ref, biased):
        s = s_ref[:, cs]
        if biased:
            q0 = cs.start % T
            s = s + bias_ref[jnp.clip(last_kb - kb, 0, N_BIAS_TILES - 1), :, q0:q0 + MXU_COLS]
            mb = jnp.max(s, axis=0, keepdims=True)
        else:
            mb = mb_ref[:, cs]
        m_prev = m_ref[:, cs]
        m_new = jnp.maximum(m_prev, mb)
        al_ref[:, cs] = jnp.exp2(m_prev - m_new)
        m_ref[:, cs] = m_new
        p_ref[:, cs] = jnp.exp2(s - m_new).astype(BF16)

    def weighted_values(j, half, cs, p_ref, al_ref):
        vt = vt_ref[j, :, half * TK:(half + 1) * TK]
        pv = jnp.dot(vt, p_ref[:, cs], preferred_element_type=F32)
        acc_ref[:, cs] = al_ref[:, cs] * acc_ref[:, cs] + pv

    def trip(j, biased):
        kb0 = per_tile * j
        for cs in col_blocks:
            scores(kb0 + 1, cs, s1_ref, mb1_ref)
            softmax(kb0, cs, s0_ref, mb0_ref, p0_ref, al0_ref, biased)
            weighted_values(j, 0, cs, p0_ref, al0_ref)
        for cs in col_blocks:
            scores(jnp.minimum(kb0 + 2, last_kb), cs, s0_ref, mb0_ref)
            softmax(kb0 + 1, cs, s1_ref, mb1_ref, p1_ref, al1_ref, biased)
            weighted_values(j, 1, cs, p1_ref, al1_ref)

    def far_trips(jj, carry):
        for u in range(FAR_UNROLL):
            trip(FAR_UNROLL * jj + u, False)
        return carry

    def near_trip(j, carry):
        trip(j, True)
        return carry

    for cs in col_blocks:
        scores(0, cs, s0_ref, mb0_ref)
    n_far = jnp.maximum(qi - 1, 0) // FAR_UNROLL
    lax.fori_loop(0, n_far, far_trips, 0)
    lax.fori_loop(FAR_UNROLL * n_far, qi + 1, near_trip, 0)

    lamv = lamv_ref[...]
    lam = (jnp.exp(jnp.sum(lamv[0:1] * lamv[1:2], axis=1, keepdims=True))
           - jnp.exp(jnp.sum(lamv[2:3] * lamv[3:4], axis=1, keepdims=True)) + lam_init)
    o_all = acc_ref[:HEAD_W, :] / acc_ref[HEAD_W:HEAD_W + 1, :]
    o = o_all[:, :T] - lam * o_all[:, T:]
    o = o * lax.rsqrt(jnp.mean(o * o, axis=0, keepdims=True) + EPS) * sg_ref[...]
    o = o * (1 - lam_init)
    o_ref[...] = o.T.astype(BF16)


def _attention(q, k, vt, bias, lamv, subln_g, *, lam_init):
    B, S, _ = q.shape
    T = SEQ_TILE
    nt = S // T
    return pl.pallas_call(
        functools.partial(_attn_kernel, lam_init=lam_init),
        grid=(B, N_HEADS, nt),
        in_specs=[
            pl.BlockSpec((None, T, HEAD_W), lambda b, h, i: (b, i, h)),
            pl.BlockSpec((None, S, HEAD_W), lambda b, h, i: (b, 0, h)),
            pl.BlockSpec((None, nt, None, VT_ROWS, T), lambda b, h, i: (b, 0, h, 0, 0)),
            pl.BlockSpec((None, N_BIAS_TILES, KV_TILE, T), lambda b, h, i: (h, 0, 0, 0)),
            pl.BlockSpec((4, HEAD_DIM), lambda b, h, i: (0, 0)),
            pl.BlockSpec((HEAD_W, 1), lambda b, h, i: (0, 0)),
        ],
        out_specs=pl.BlockSpec((None, T, HEAD_W), lambda b, h, i: (b, i, h)),
        out_shape=jax.ShapeDtypeStruct((B, S, ATTN_W), BF16),
        scratch_shapes=[
            pltpu.VMEM((2 * T, HEAD_W), BF16),
            pltpu.VMEM((KV_TILE, 2 * T), F32),
            pltpu.VMEM((KV_TILE, 2 * T), F32),
            pltpu.VMEM((1, 2 * T), F32),
            pltpu.VMEM((1, 2 * T), F32),
            pltpu.VMEM((KV_TILE, 2 * T), BF16),
            pltpu.VMEM((KV_TILE, 2 * T), BF16),
            pltpu.VMEM((1, 2 * T), F32),
            pltpu.VMEM((1, 2 * T), F32),
            pltpu.VMEM((1, 2 * T), F32),
            pltpu.VMEM((VT_ROWS, 2 * T), F32),
        ],
        compiler_params=_params("parallel", "parallel", "arbitrary"),
        name="diff_attn",
    )(q, k, vt, bias, lamv, subln_g.reshape(HEAD_W, 1))


def _conv_kernel(z_ref, zprev_ref, w8_ref, b_ref, lng_ref, lnb_ref, o_ref, zext_ref, y_ref):
    T = SEQ_TILE
    i = pl.program_id(1)
    prev = zprev_ref[...]
    zext_ref[0:CONV_HALO, :] = jnp.where(i == 0, jnp.zeros_like(prev), prev)
    zext_ref[CONV_HALO:, :] = z_ref[...]

    halo_groups = CONV_HALO // SUBLANES
    n_a = (CONV_K - 1) // SUBLANES + 1
    sub = lax.broadcasted_iota(jnp.int32, (SUBLANES, LANES), 0)

    for c0 in range(0, z_ref.shape[1], LANES):
        cs = slice(c0, c0 + LANES)
        w = [w8_ref[SUBLANES * j:SUBLANES * (j + 1), cs] for j in range(CONV_K)]
        bias = b_ref[:, cs]

        def partial_sums(zs):
            ys = []
            for b in range(SUBLANES):
                acc = None
                for a in range(n_a):
                    s = SUBLANES * a + b
                    if s < CONV_K:
                        t = w[CONV_K - 1 - s] * zs[a]
                        acc = t if acc is None else acc + t
                ys.append(acc)
            return ys

        def group(g):
            return zext_ref[pl.ds(pl.multiple_of(g * SUBLANES, SUBLANES), SUBLANES), cs]

        zs0 = [zext_ref[(halo_groups - 1 - a) * SUBLANES:(halo_groups - a) * SUBLANES, cs]
               for a in range(n_a)]
        ys0 = partial_sums(zs0)
        rolled0 = tuple(pltpu.roll(ys0[b], b, 0) for b in range(1, SUBLANES))

        def body(g, carry):
            rolled_prev, z_hist = carry
            zs = (group(g + halo_groups),) + z_hist
            ys = partial_sums(zs)
            out = ys[0] + bias
            rolled = []
            for b in range(1, SUBLANES):
                r = pltpu.roll(ys[b], b, 0)
                out = out + jnp.where(sub >= b, r, rolled_prev[b - 1])
                rolled.append(r)
            y_ref[pl.ds(pl.multiple_of(g * SUBLANES, SUBLANES), SUBLANES), cs] = out
            return tuple(rolled), zs[:n_a - 1]

        lax.fori_loop(0, T // SUBLANES, body, (rolled0, tuple(zs0[:n_a - 1])), unroll=2)

    rows = 32

    def ln_body(r, carry):
        rs = pl.ds(pl.multiple_of(r * rows, rows), rows)
        acc = y_ref[rs, :]
        mu = jnp.mean(acc, axis=-1, keepdims=True)
        xc = acc - mu
        y = xc * lax.rsqrt(jnp.mean(xc * xc, axis=-1, keepdims=True) + EPS)
        y = y * lng_ref[...] + lnb_ref[...]
        o_ref[rs, :] = (y * jax.nn.sigmoid(y)).astype(BF16)
        return carry

    lax.fori_loop(0, T // rows, ln_body, 0, unroll=4)


def _conv_module(z, w_dw, b_dw, ln_g, ln_b):
    B, S, C = z.shape
    T = SEQ_TILE
    halo_blocks = T // CONV_HALO
    vec = lambda a: a.reshape(1, C)
    w8 = jnp.repeat(w_dw, SUBLANES, axis=0)
    return pl.pallas_call(
        _conv_kernel,
        grid=(B, S // T),
        in_specs=[
            pl.BlockSpec((None, T, C), lambda b, i: (b, i, 0)),
            pl.BlockSpec((None, CONV_HALO, C), lambda b, i: (b, jnp.maximum(i * halo_blocks - 1, 0), 0)),
            _resident((CONV_K * SUBLANES, C)),
            _resident((1, C)),
            _resident((1, C)),
            _resident((1, C)),
        ],
        out_specs=pl.BlockSpec((None, T, C), lambda b, i: (b, i, 0)),
        out_shape=jax.ShapeDtypeStruct((B, S, C), BF16),
        scratch_shapes=[pltpu.VMEM((T + CONV_HALO, C), F32), pltpu.VMEM((T, C), F32)],
        compiler_params=_params("parallel", "parallel"),
        name="conv_module",
    )(z, z, w8, vec(b_dw), vec(ln_g), vec(ln_b))


def _mix_out_kernel(x_ref, mod_ref, o_ref, cact_ref, gates_ref, w_ab_ref, w_cb_ref, w_mo_ref, y_ref):
    y_attn = jnp.dot(o_ref[...], w_ab_ref[...], preferred_element_type=F32)
    y_conv = jnp.dot(cact_ref[...], w_cb_ref[...], preferred_element_type=F32)
    merged = gates_ref[:, :D_MODEL] * y_attn + gates_ref[:, D_MODEL:] * y_conv
    out = jnp.dot(merged.astype(BF16), w_mo_ref[...], preferred_element_type=F32)
    y_ref[...] = x_ref[...] + mod_ref[5:6, :] * out


def _mix_out(x, mod_l, o, cact, gates, w_ab, w_cb, w_mo):
    B, S, _ = x.shape
    tile = pl.BlockSpec((None, SEQ_TILE, D_MODEL), lambda b, i: (b, i, 0))
    return pl.pallas_call(
        _mix_out_kernel,
        grid=(B, S // SEQ_TILE),
        in_specs=[
            tile,
            pl.BlockSpec((None, N_MOD, D_MODEL), lambda b, i: (b, 0, 0)),
            tile,
            tile,
            pl.BlockSpec((None, SEQ_TILE, 2 * D_MODEL), lambda b, i: (b, i, 0)),
            _resident((D_MODEL, D_MODEL)),
            _resident((D_MODEL, D_MODEL)),
            _resident((D_MODEL, D_MODEL)),
        ],
        out_specs=tile,
        out_shape=jax.ShapeDtypeStruct(x.shape, F32),
        compiler_params=_params("parallel", "parallel"),
        name="mix_out",
    )(x, mod_l, o, cact, gates, w_ab, w_cb, w_mo)


def kernel(x, c, ffn1_norm, ffn1_w_in, ffn1_w_out, mix_norm, w_mix_in, lambda_q1, lambda_k1, lambda_q2, lambda_k2, subln_gain, w_attn_branch, conv_dw, conv_dw_bias, conv_ln_gain, conv_ln_bias, w_conv_branch, w_mix_out, ffn2_norm, ffn2_w_in, ffn2_w_out, ada_w, ada_b, rel_bias, final_norm):
    B, S, _ = x.shape
    assert S % SEQ_TILE == 0 and SEQ_TILE % CONV_HALO == 0 and CONV_HALO >= CONV_K - 1
    assert D_FF % FF_CHUNK == 0
    assert SEQ_TILE == 2 * KV_TILE and KV_TILE >= MAX_DISTANCE

    mod_rows = 8
    c_pad = jnp.zeros((mod_rows, D_MODEL), F32).at[:B].set(c)
    mod = _modulation(c_pad, ada_w, ada_b)[:, :B].reshape(DEPTH, B, N_MOD, D_MODEL)
    bias = _bias_tiles(rel_bias)

    bf = lambda w: w.astype(BF16)
    for l in range(DEPTH):
        lam_init = 0.8 - 0.6 * math.exp(-0.3 * l)
        mod_l = mod[l]
        x = _ffn(x, mod_l, ffn1_norm[l], bf(ffn1_w_in[l]), bf(ffn1_w_out[l]), final_norm,
                 mod_row=0, final=False)

        w_mix = w_mix_in[l]
        q, k, vt, z, gates = _mix_in(
            x, mod_l, mix_norm[l],
            bf(w_mix[:, :2 * ATTN_W]),
            bf(w_mix[:, 2 * ATTN_W:3 * ATTN_W].T),
            bf(w_mix[:, 3 * ATTN_W:3 * ATTN_W + 2 * D_MODEL]),
            bf(w_mix[:, 3 * ATTN_W + 2 * D_MODEL:]))
        lamv = jnp.stack([lambda_q1[l], lambda_k1[l], lambda_q2[l], lambda_k2[l]]).astype(F32)
        o = _attention(q, k, vt, bias, lamv, subln_gain[l], lam_init=lam_init)
        cact = _conv_module(z, conv_dw[l], conv_dw_bias[l], conv_ln_gain[l], conv_ln_bias[l])
        x = _mix_out(x, mod_l, o, cact, gates, bf(w_attn_branch[l]), bf(w_conv_branch[l]),
                     bf(w_mix_out[l]))

        x = _ffn(x, mod_l, ffn2_norm[l], bf(ffn2_w_in[l]), bf(ffn2_w_out[l]), final_norm,
                 mod_row=6, final=(l == DEPTH - 1))
    return x
```

```python
import functools
import math

import jax
import jax.numpy as jnp
from jax import lax
from jax.experimental import pallas as pl
from jax.experimental.pallas import tpu as pltpu

D_MODEL = 1024
DEPTH = 2
N_HEADS = 8
HEAD_DIM = 64
HEAD_W = 2 * HEAD_DIM
ATTN_W = N_HEADS * HEAD_W
CONV_K = 31
D_FF = 2816
NUM_BUCKETS = 32
MAX_DISTANCE = 128
N_MOD = 9
EPS = 1e-6

SUBLANES = 8
LANES = 128
MXU_COLS = 256

SEQ_TILE = 512
KV_TILE = 256
FAR_UNROLL = 4
N_BIAS_TILES = 4
LOG2E = math.log2(math.e)
BF16_SUBLANES = 16
VT_ROWS = HEAD_W + BF16_SUBLANES
CONV_HALO = 32
FF_CHUNK = 1408
MASK_BIAS = -1e30
VMEM_LIMIT = 56 * 1024 * 1024

F32 = jnp.float32
BF16 = jnp.bfloat16


def _params(*sem):
    return pltpu.CompilerParams(dimension_semantics=sem, vmem_limit_bytes=VMEM_LIMIT)


def _resident(shape):
    return pl.BlockSpec(shape, lambda *_: (0,) * len(shape), pipeline_mode=pl.Buffered(1))


def _rms_norm(x, g):
    return x * lax.rsqrt(jnp.mean(x * x, axis=-1, keepdims=True) + EPS) * g


def _mod_kernel(c_ref, w_ref, b_ref, o_ref):
    c = c_ref[...]
    c_act = (c * jax.nn.sigmoid(c)).astype(BF16)
    o_ref[...] = jnp.dot(c_act, w_ref[...].astype(BF16), preferred_element_type=F32) + b_ref[...]


def _modulation(c_pad, ada_w, ada_b):
    rows = c_pad.shape[0]
    return pl.pallas_call(
        _mod_kernel,
        grid=(DEPTH, N_MOD),
        in_specs=[
            pl.BlockSpec((rows, D_MODEL), lambda l, j: (0, 0)),
            pl.BlockSpec((None, D_MODEL, D_MODEL), lambda l, j: (l, 0, j)),
            pl.BlockSpec((None, 1, D_MODEL), lambda l, j: (l, 0, j)),
        ],
        out_specs=pl.BlockSpec((None, rows, D_MODEL), lambda l, j: (l, 0, j)),
        out_shape=jax.ShapeDtypeStruct((DEPTH, rows, N_MOD * D_MODEL), F32),
        compiler_params=_params("arbitrary", "arbitrary"),
        name="adaln_mod",
    )(c_pad, ada_w, ada_b.reshape(DEPTH, 1, N_MOD * D_MODEL))


def _ffn_kernel(x_ref, mod_ref, g_ref, w_in_ref, w_out_ref, fg_ref, o_ref, *, mod_row, final):
    x = x_ref[...]
    shift = mod_ref[mod_row:mod_row + 1, :]
    scale = mod_ref[mod_row + 1:mod_row + 2, :]
    gate = mod_ref[mod_row + 2:mod_row + 3, :]
    h = (_rms_norm(x, g_ref[...]) * (1 + scale) + shift).astype(BF16)
    out = None
    for c0 in range(0, D_FF, FF_CHUNK):
        g = jnp.dot(h, w_in_ref[:, c0:c0 + FF_CHUNK], preferred_element_type=F32)
        u = jnp.dot(h, w_in_ref[:, D_FF + c0:D_FF + c0 + FF_CHUNK], preferred_element_type=F32)
        act = (g * jax.nn.sigmoid(g) * u).astype(BF16)
        part = jnp.dot(act, w_out_ref[c0:c0 + FF_CHUNK, :], preferred_element_type=F32)
        out = part if out is None else out + part
    y = x + (0.5 * gate) * out
    if final:
        y = _rms_norm(y, fg_ref[...])
    o_ref[...] = y


def _ffn(x, mod_l, norm_g, w_in, w_out, final_g, *, mod_row, final):
    B, S, _ = x.shape
    tile = pl.BlockSpec((None, SEQ_TILE, D_MODEL), lambda b, i: (b, i, 0))
    return pl.pallas_call(
        functools.partial(_ffn_kernel, mod_row=mod_row, final=final),
        grid=(B, S // SEQ_TILE),
        in_specs=[
            tile,
            pl.BlockSpec((None, N_MOD, D_MODEL), lambda b, i: (b, 0, 0)),
            _resident((1, D_MODEL)),
            _resident((D_MODEL, 2 * D_FF)),
            _resident((D_FF, D_MODEL)),
            _resident((1, D_MODEL)),
        ],
        out_specs=tile,
        out_shape=jax.ShapeDtypeStruct(x.shape, F32),
        compiler_params=_params("parallel", "parallel"),
        name="ffn",
    )(x, mod_l, norm_g.reshape(1, D_MODEL), w_in, w_out, final_g.reshape(1, D_MODEL))


def _mix_in_kernel(x_ref, mod_ref, g_ref, w_qk_ref, w_vt_ref, w_u_ref, w_g_ref,
                   q_ref, k_ref, vt_ref, z_ref, gates_ref):
    x = x_ref[...]
    shift = mod_ref[3:4, :]
    scale = mod_ref[4:5, :]
    h = (_rms_norm(x, g_ref[...]) * (1 + scale) + shift).astype(BF16)
    q = jnp.dot(h, w_qk_ref[:, :ATTN_W], preferred_element_type=F32)
    q_ref[...] = (q * (HEAD_DIM ** -0.5 * LOG2E)).astype(BF16)
    k_ref[...] = jnp.dot(h, w_qk_ref[:, ATTN_W:], preferred_element_type=F32).astype(BF16)
    vt = lax.dot_general(w_vt_ref[...], h, (((1,), (1,)), ((), ())), preferred_element_type=F32)
    vt_ref[:, :HEAD_W, :] = vt.astype(BF16).reshape(N_HEADS, HEAD_W, SEQ_TILE)
    vt_ref[:, HEAD_W:, :] = jnp.ones((N_HEADS, VT_ROWS - HEAD_W, SEQ_TILE), BF16)
    a = jnp.dot(h, w_u_ref[:, :D_MODEL], preferred_element_type=F32)
    g = jnp.dot(h, w_u_ref[:, D_MODEL:], preferred_element_type=F32)
    z_ref[...] = a * jax.nn.sigmoid(g)
    gates_ref[...] = jax.nn.sigmoid(jnp.dot(h, w_g_ref[...], preferred_element_type=F32))


def _mix_in(x, mod_l, norm_g, w_qk, w_vt, w_u, w_g):
    B, S, _ = x.shape
    nt = S // SEQ_TILE
    tile = pl.BlockSpec((None, SEQ_TILE, D_MODEL), lambda b, i: (b, i, 0))
    return pl.pallas_call(
        _mix_in_kernel,
        grid=(B, nt),
        in_specs=[
            tile,
            pl.BlockSpec((None, N_MOD, D_MODEL), lambda b, i: (b, 0, 0)),
            _resident((1, D_MODEL)),
            _resident((D_MODEL, 2 * ATTN_W)),
            _resident((ATTN_W, D_MODEL)),
            _resident((D_MODEL, 2 * D_MODEL)),
            _resident((D_MODEL, 2 * D_MODEL)),
        ],
        out_specs=[
            tile,
            tile,
            pl.BlockSpec((None, None, N_HEADS, VT_ROWS, SEQ_TILE), lambda b, i: (b, i, 0, 0, 0)),
            tile,
            pl.BlockSpec((None, SEQ_TILE, 2 * D_MODEL), lambda b, i: (b, i, 0)),
        ],
        out_shape=[
            jax.ShapeDtypeStruct((B, S, ATTN_W), BF16),
            jax.ShapeDtypeStruct((B, S, ATTN_W), BF16),
            jax.ShapeDtypeStruct((B, nt, N_HEADS, VT_ROWS, SEQ_TILE), BF16),
            jax.ShapeDtypeStruct((B, S, D_MODEL), F32),
            jax.ShapeDtypeStruct((B, S, 2 * D_MODEL), F32),
        ],
        compiler_params=_params("parallel", "parallel"),
        name="mix_in",
    )(x, mod_l, norm_g.reshape(1, D_MODEL), w_qk, w_vt, w_u, w_g)


def _bias_kernel(table_ref, o_ref):
    h = pl.program_id(0)
    t = pl.program_id(1)
    shape = (KV_TILE, SEQ_TILE)
    dist = (lax.broadcasted_iota(jnp.int32, shape, 1) - lax.broadcasted_iota(jnp.int32, shape, 0)
            + (t - 1) * KV_TILE)
    n = jnp.maximum(dist, 0)
    max_exact = NUM_BUCKETS // 2
    nf = jnp.maximum(n, 1).astype(F32)
    large = max_exact + (jnp.log(nf / max_exact) / math.log(MAX_DISTANCE / max_exact)
                         * (NUM_BUCKETS - max_exact)).astype(jnp.int32)
    large = jnp.minimum(large, NUM_BUCKETS - 1)
    bucket = jnp.where(n < max_exact, n, large)
    far = table_ref[h, NUM_BUCKETS - 1]
    val = jnp.zeros(shape, F32)
    for b in range(NUM_BUCKETS - 1):
        val = jnp.where(bucket == b, (table_ref[h, b] - far) * LOG2E, val)
    o_ref[...] = jnp.where(dist >= 0, val, MASK_BIAS)


def _bias_tiles(rel_bias):
    return pl.pallas_call(
        _bias_kernel,
        grid=(N_HEADS, N_BIAS_TILES),
        in_specs=[pl.BlockSpec(memory_space=pltpu.SMEM)],
        out_specs=pl.BlockSpec((None, None, KV_TILE, SEQ_TILE), lambda h, t: (h, t, 0, 0)),
        out_shape=jax.ShapeDtypeStruct((N_HEADS, N_BIAS_TILES, KV_TILE, SEQ_TILE), F32),
        compiler_params=_params("arbitrary", "arbitrary"),
        name="rel_bias_tiles",
    )(rel_bias)


def _attn_kernel(q_ref, k_ref, vt_ref, bias_ref, lamv_ref, sg_ref, o_ref,
                 qs_ref, s0_ref, s1_ref, mb0_ref, mb1_ref, p0_ref, p1_ref, al0_ref, al1_ref,
                 m_ref, acc_ref, *, lam_init):
    T = SEQ_TILE
    TK = KV_TILE
    per_tile = T // TK
    qi = pl.program_id(2)
    last_kb = per_tile * qi + (per_tile - 1)

    q = q_ref[...]
    lane = lax.broadcasted_iota(jnp.int32, q.shape, 1)
    zero = jnp.zeros_like(q)
    qs_ref[0:T, :] = jnp.where(lane < HEAD_DIM, q, zero)
    qs_ref[T:2 * T, :] = jnp.where(lane >= HEAD_DIM, q, zero)

    m_ref[...] = jnp.full(m_ref.shape, MASK_BIAS, F32)
    acc_ref[...] = jnp.zeros(acc_ref.shape, F32)

    n_cb = 2 * T // MXU_COLS

    def scores(kb, c, s_ref, mb_ref):
        kblk = k_ref[pl.ds(pl.multiple_of(kb * TK, TK), TK), :]
        s = lax.dot_general(kblk, qs_ref[c * MXU_COLS:(c + 1) * MXU_COLS, :], (((1,), (1,)), ((), ())),
                            preferred_element_type=F32)
        mx = None
        for r in range(0, TK, SUBLANES):
            chunk = s[r:r + SUBLANES, :]
            s_ref[c, r:r + SUBLANES, :] = chunk
            mx = chunk if mx is None else jnp.maximum(mx, chunk)
        mb_ref[c] = jnp.max(mx, axis=0, keepdims=True)

    def softmax(kb, c, s_ref, mb_ref, p_ref, al_ref, biased):
        s = s_ref[c]
        if biased:
            q0 = c * MXU_COLS % T
            s = s + bias_ref[jnp.clip(last_kb - kb, 0, N_BIAS_TILES - 1), :, q0:q0 + MXU_COLS]
            mb = jnp.max(s, axis=0, keepdims=True)
        else:
            mb = mb_ref[c]
        m_prev = m_ref[c]
        m_new = jnp.maximum(m_prev, mb)
        al_ref[c] = jnp.exp2(m_prev - m_new)
        m_ref[c] = m_new
        p_ref[c] = jnp.exp2(s - m_new).astype(BF16)

    def weighted_values(j, half, c, p_ref, al_ref):
        vt = vt_ref[j, :, half * TK:(half + 1) * TK]
        pv = jnp.dot(vt, p_ref[c], preferred_element_type=F32)
        acc_ref[c] = al_ref[c] * acc_ref[c] + pv

    def trip(j, biased):
        kb0 = per_tile * j
        for c in range(n_cb):
            scores(kb0 + 1, c, s1_ref, mb1_ref)
            softmax(kb0, c, s0_ref, mb0_ref, p0_ref, al0_ref, biased)
            weighted_values(j, 0, c, p0_ref, al0_ref)
        for c in range(n_cb):
            scores(jnp.minimum(kb0 + 2, last_kb), c, s0_ref, mb0_ref)
            softmax(kb0 + 1, c, s1_ref, mb1_ref, p1_ref, al1_ref, biased)
            weighted_values(j, 1, c, p1_ref, al1_ref)

    def far_trips(jj, carry):
        for u in range(FAR_UNROLL):
            trip(FAR_UNROLL * jj + u, False)
        return carry

    def far_trip(j, carry):
        trip(j, False)
        return carry

    for c in range(n_cb):
        scores(0, c, s0_ref, mb0_ref)
    n_far = jnp.maximum(qi - 1, 0)
    n_unrolled = n_far // FAR_UNROLL
    lax.fori_loop(0, n_unrolled, far_trips, 0)
    lax.fori_loop(FAR_UNROLL * n_unrolled, n_far, far_trip, 0)

    @pl.when(qi >= 1)
    def _():
        trip(qi - 1, True)
        trip(qi, True)

    @pl.when(qi == 0)
    def _():
        trip(qi, True)

    lamv = lamv_ref[...]
    lam = (jnp.exp(jnp.sum(lamv[0:1] * lamv[1:2], axis=1, keepdims=True))
           - jnp.exp(jnp.sum(lamv[2:3] * lamv[3:4], axis=1, keepdims=True)) + lam_init)

    def normalised(c):
        return acc_ref[c, :HEAD_W, :] / acc_ref[c, HEAD_W:HEAD_W + 1, :]

    half_cb = n_cb // 2
    o = jnp.concatenate([normalised(c) - lam * normalised(c + half_cb) for c in range(half_cb)],
                        axis=1)
    o = o * lax.rsqrt(jnp.mean(o * o, axis=0, keepdims=True) + EPS) * sg_ref[...]
    o = o * (1 - lam_init)
    o_ref[...] = o.T.astype(BF16)


def _attention(q, k, vt, bias, lamv, subln_g, *, lam_init):
    B, S, _ = q.shape
    T = SEQ_TILE
    nt = S // T
    n_cb = 2 * T // MXU_COLS
    return pl.pallas_call(
        functools.partial(_attn_kernel, lam_init=lam_init),
        grid=(B, N_HEADS, nt),
        in_specs=[
            pl.BlockSpec((None, T, HEAD_W), lambda b, h, i: (b, i, h)),
            pl.BlockSpec((None, S, HEAD_W), lambda b, h, i: (b, 0, h)),
            pl.BlockSpec((None, nt, None, VT_ROWS, T), lambda b, h, i: (b, 0, h, 0, 0)),
            pl.BlockSpec((None, N_BIAS_TILES, KV_TILE, T), lambda b, h, i: (h, 0, 0, 0)),
            pl.BlockSpec((4, HEAD_DIM), lambda b, h, i: (0, 0)),
            pl.BlockSpec((HEAD_W, 1), lambda b, h, i: (0, 0)),
        ],
        out_specs=pl.BlockSpec((None, T, HEAD_W), lambda b, h, i: (b, i, h)),
        out_shape=jax.ShapeDtypeStruct((B, S, ATTN_W), BF16),
        scratch_shapes=[
            pltpu.VMEM((2 * T, HEAD_W), BF16),
            pltpu.VMEM((n_cb, KV_TILE, MXU_COLS), F32),
            pltpu.VMEM((n_cb, KV_TILE, MXU_COLS), F32),
            pltpu.VMEM((n_cb, 1, MXU_COLS), F32),
            pltpu.VMEM((n_cb, 1, MXU_COLS), F32),
            pltpu.VMEM((n_cb, KV_TILE, MXU_COLS), BF16),
            pltpu.VMEM((n_cb, KV_TILE, MXU_COLS), BF16),
            pltpu.VMEM((n_cb, 1, MXU_COLS), F32),
            pltpu.VMEM((n_cb, 1, MXU_COLS), F32),
            pltpu.VMEM((n_cb, 1, MXU_COLS), F32),
            pltpu.VMEM((n_cb, VT_ROWS, MXU_COLS), F32),
        ],
        compiler_params=_params("parallel", "parallel", "arbitrary"),
        name="diff_attn",
    )(q, k, vt, bias, lamv, subln_g.reshape(HEAD_W, 1))


def _conv_kernel(z_ref, zprev_ref, w8_ref, b_ref, lng_ref, lnb_ref, o_ref, zext_ref, y_ref):
    T = SEQ_TILE
    i = pl.program_id(1)
    prev = zprev_ref[...]
    zext_ref[0:CONV_HALO, :] = jnp.where(i == 0, jnp.zeros_like(prev), prev)
    zext_ref[CONV_HALO:, :] = z_ref[...]

    halo_groups = CONV_HALO // SUBLANES
    n_a = (CONV_K - 1) // SUBLANES + 1
    sub = lax.broadcasted_iota(jnp.int32, (SUBLANES, LANES), 0)

    for c0 in range(0, z_ref.shape[1], LANES):
        cs = slice(c0, c0 + LANES)
        w = [w8_ref[SUBLANES * j:SUBLANES * (j + 1), cs] for j in range(CONV_K)]
        bias = b_ref[:, cs]

        def partial_sums(zs):
            ys = []
            for b in range(SUBLANES):
                acc = None
                for a in range(n_a):
                    s = SUBLANES * a + b
                    if s < CONV_K:
                        t = w[CONV_K - 1 - s] * zs[a]
                        acc = t if acc is None else acc + t
                ys.append(acc)
            return ys

        def group(g):
            return zext_ref[pl.ds(pl.multiple_of(g * SUBLANES, SUBLANES), SUBLANES), cs]

        zs0 = [zext_ref[(halo_groups - 1 - a) * SUBLANES:(halo_groups - a) * SUBLANES, cs]
               for a in range(n_a)]
        ys0 = partial_sums(zs0)
        rolled0 = tuple(pltpu.roll(ys0[b], b, 0) for b in range(1, SUBLANES))

        def body(g, carry):
            rolled_prev, z_hist = carry
            zs = (group(g + halo_groups),) + z_hist
            ys = partial_sums(zs)
            out = ys[0] + bias
            rolled = []
            for b in range(1, SUBLANES):
                r = pltpu.roll(ys[b], b, 0)
                out = out + jnp.where(sub >= b, r, rolled_prev[b - 1])
                rolled.append(r)
            y_ref[pl.ds(pl.multiple_of(g * SUBLANES, SUBLANES), SUBLANES), cs] = out
            return tuple(rolled), zs[:n_a - 1]

        lax.fori_loop(0, T // SUBLANES, body, (rolled0, tuple(zs0[:n_a - 1])), unroll=2)

    rows = 32

    def ln_body(r, carry):
        rs = pl.ds(pl.multiple_of(r * rows, rows), rows)
        acc = y_ref[rs, :]
        mu = jnp.mean(acc, axis=-1, keepdims=True)
        xc = acc - mu
        y = xc * lax.rsqrt(jnp.mean(xc * xc, axis=-1, keepdims=True) + EPS)
        y = y * lng_ref[...] + lnb_ref[...]
        o_ref[rs, :] = (y * jax.nn.sigmoid(y)).astype(BF16)
        return carry

    lax.fori_loop(0, T // rows, ln_body, 0, unroll=4)


def _conv_module(z, w_dw, b_dw, ln_g, ln_b):
    B, S, C = z.shape
    T = SEQ_TILE
    halo_blocks = T // CONV_HALO
    vec = lambda a: a.reshape(1, C)
    w8 = jnp.repeat(w_dw, SUBLANES, axis=0)
    return pl.pallas_call(
        _conv_kernel,
        grid=(B, S // T),
        in_specs=[
            pl.BlockSpec((None, T, C), lambda b, i: (b, i, 0)),
            pl.BlockSpec((None, CONV_HALO, C), lambda b, i: (b, jnp.maximum(i * halo_blocks - 1, 0), 0)),
            _resident((CONV_K * SUBLANES, C)),
            _resident((1, C)),
            _resident((1, C)),
            _resident((1, C)),
        ],
        out_specs=pl.BlockSpec((None, T, C), lambda b, i: (b, i, 0)),
        out_shape=jax.ShapeDtypeStruct((B, S, C), BF16),
        scratch_shapes=[pltpu.VMEM((T + CONV_HALO, C), F32), pltpu.VMEM((T, C), F32)],
        compiler_params=_params("parallel", "parallel"),
        name="conv_module",
    )(z, z, w8, vec(b_dw), vec(ln_g), vec(ln_b))


def _mix_out_kernel(x_ref, mod_ref, o_ref, cact_ref, gates_ref, w_ab_ref, w_cb_ref, w_mo_ref, y_ref):
    y_attn = jnp.dot(o_ref[...], w_ab_ref[...], preferred_element_type=F32)
    y_conv = jnp.dot(cact_ref[...], w_cb_ref[...], preferred_element_type=F32)
    merged = gates_ref[:, :D_MODEL] * y_attn + gates_ref[:, D_MODEL:] * y_conv
    out = jnp.dot(merged.astype(BF16), w_mo_ref[...], preferred_element_type=F32)
    y_ref[...] = x_ref[...] + mod_ref[5:6, :] * out


def _mix_out(x, mod_l, o, cact, gates, w_ab, w_cb, w_mo):
    B, S, _ = x.shape
    tile = pl.BlockSpec((None, SEQ_TILE, D_MODEL), lambda b, i: (b, i, 0))
    return pl.pallas_call(
        _mix_out_kernel,
        grid=(B, S // SEQ_TILE),
        in_specs=[
            tile,
            pl.BlockSpec((None, N_MOD, D_MODEL), lambda b, i: (b, 0, 0)),
            tile,
            tile,
            pl.BlockSpec((None, SEQ_TILE, 2 * D_MODEL), lambda b, i: (b, i, 0)),
            _resident((D_MODEL, D_MODEL)),
            _resident((D_MODEL, D_MODEL)),
            _resident((D_MODEL, D_MODEL)),
        ],
        out_specs=tile,
        out_shape=jax.ShapeDtypeStruct(x.shape, F32),
        compiler_params=_params("parallel", "parallel"),
        name="mix_out",
    )(x, mod_l, o, cact, gates, w_ab, w_cb, w_mo)


def kernel(x, c, ffn1_norm, ffn1_w_in, ffn1_w_out, mix_norm, w_mix_in, lambda_q1, lambda_k1, lambda_q2, lambda_k2, subln_gain, w_attn_branch, conv_dw, conv_dw_bias, conv_ln_gain, conv_ln_bias, w_conv_branch, w_mix_out, ffn2_norm, ffn2_w_in, ffn2_w_out, ada_w, ada_b, rel_bias, final_norm):
    B, S, _ = x.shape
    assert S % SEQ_TILE == 0 and SEQ_TILE % CONV_HALO == 0 and CONV_HALO >= CONV_K - 1
    assert D_FF % FF_CHUNK == 0
    assert SEQ_TILE == 2 * KV_TILE and KV_TILE >= MAX_DISTANCE

    mod_rows = 8
    c_pad = jnp.zeros((mod_rows, D_MODEL), F32).at[:B].set(c)
    mod = _modulation(c_pad, ada_w, ada_b)[:, :B].reshape(DEPTH, B, N_MOD, D_MODEL)
    bias = _bias_tiles(rel_bias)

    bf = lambda w: w.astype(BF16)
    for l in range(DEPTH):
        lam_init = 0.8 - 0.6 * math.exp(-0.3 * l)
        mod_l = mod[l]
        x = _ffn(x, mod_l, ffn1_norm[l], bf(ffn1_w_in[l]), bf(ffn1_w_out[l]), final_norm,
                 mod_row=0, final=False)

        w_mix = w_mix_in[l]
        q, k, vt, z, gates = _mix_in(
            x, mod_l, mix_norm[l],
            bf(w_mix[:, :2 * ATTN_W]),
            bf(w_mix[:, 2 * ATTN_W:3 * ATTN_W].T),
            bf(w_mix[:, 3 * ATTN_W:3 * ATTN_W + 2 * D_MODEL]),
            bf(w_mix[:, 3 * ATTN_W + 2 * D_MODEL:]))
        lamv = jnp.stack([lambda_q1[l], lambda_k1[l], lambda_q2[l], lambda_k2[l]]).astype(F32)
        o = _attention(q, k, vt, bias, lamv, subln_gain[l], lam_init=lam_init)
        cact = _conv_module(z, conv_dw[l], conv_dw_bias[l], conv_ln_gain[l], conv_ln_bias[l])
        x = _mix_out(x, mod_l, o, cact, gates, bf(w_attn_branch[l]), bf(w_conv_branch[l]),
                     bf(w_mix_out[l]))

        x = _ffn(x, mod_l, ffn2_norm[l], bf(ffn2_w_in[l]), bf(ffn2_w_out[l]), final_norm,
                 mod_row=6, final=(l == DEPTH - 1))
    return x
```

```python
import functools
import math

import jax
import jax.numpy as jnp
from jax import lax
from jax.experimental import pallas as pl
from jax.experimental.pallas import tpu as pltpu

D_MODEL = 1024
DEPTH = 2
N_HEADS = 8
HEAD_DIM = 64
HEAD_W = 2 * HEAD_DIM
ATTN_W = N_HEADS * HEAD_W
CONV_K = 31
D_FF = 2816
NUM_BUCKETS = 32
MAX_DISTANCE = 128
N_MOD = 9
EPS = 1e-6

SUBLANES = 8
LANES = 128
MXU_COLS = 256

SEQ_TILE = 512
KV_TILE = 256
FAR_UNROLL = 4
N_BIAS_TILES = 4
LOG2E = math.log2(math.e)
BF16_SUBLANES = 16
VT_ROWS = HEAD_W + BF16_SUBLANES
CONV_HALO = 32
FF_CHUNKS = ((0, 5 * MXU_COLS), (5 * MXU_COLS, D_FF))
MASK_BIAS = -1e30
VMEM_LIMIT = 56 * 1024 * 1024

F32 = jnp.float32
BF16 = jnp.bfloat16


def _params(*sem):
    return pltpu.CompilerParams(dimension_semantics=sem, vmem_limit_bytes=VMEM_LIMIT)


def _resident(shape):
    return pl.BlockSpec(shape, lambda *_: (0,) * len(shape), pipeline_mode=pl.Buffered(1))


def _rms_norm(x, g):
    return x * lax.rsqrt(jnp.mean(x * x, axis=-1, keepdims=True) + EPS) * g


def _mod_kernel(c_ref, w_ref, b_ref, o_ref):
    c = c_ref[...]
    c_act = (c * jax.nn.sigmoid(c)).astype(BF16)
    o_ref[...] = jnp.dot(c_act, w_ref[...].astype(BF16), preferred_element_type=F32) + b_ref[...]


def _modulation(c_pad, ada_w, ada_b):
    rows = c_pad.shape[0]
    return pl.pallas_call(
        _mod_kernel,
        grid=(DEPTH, N_MOD),
        in_specs=[
            pl.BlockSpec((rows, D_MODEL), lambda l, j: (0, 0)),
            pl.BlockSpec((None, D_MODEL, D_MODEL), lambda l, j: (l, 0, j)),
            pl.BlockSpec((None, 1, D_MODEL), lambda l, j: (l, 0, j)),
        ],
        out_specs=pl.BlockSpec((None, rows, D_MODEL), lambda l, j: (l, 0, j)),
        out_shape=jax.ShapeDtypeStruct((DEPTH, rows, N_MOD * D_MODEL), F32),
        compiler_params=_params("arbitrary", "arbitrary"),
        name="adaln_mod",
    )(c_pad, ada_w, ada_b.reshape(DEPTH, 1, N_MOD * D_MODEL))


def _ffn_kernel(x_ref, mod_ref, g_ref, w_in_ref, w_out_ref, fg_ref, o_ref, *, mod_row, final):
    x = x_ref[...]
    shift = mod_ref[mod_row:mod_row + 1, :]
    scale = mod_ref[mod_row + 1:mod_row + 2, :]
    gate = mod_ref[mod_row + 2:mod_row + 3, :]
    h = (_rms_norm(x, g_ref[...]) * (1 + scale) + shift).astype(BF16)
    out = None
    for c0, c1 in FF_CHUNKS:
        g = jnp.dot(h, w_in_ref[:, c0:c1], preferred_element_type=F32)
        u = jnp.dot(h, w_in_ref[:, D_FF + c0:D_FF + c1], preferred_element_type=F32)
        act = (g * jax.nn.sigmoid(g) * u).astype(BF16)
        part = jnp.dot(act, w_out_ref[c0:c1, :], preferred_element_type=F32)
        out = part if out is None else out + part
    y = x + (0.5 * gate) * out
    if final:
        y = _rms_norm(y, fg_ref[...])
    o_ref[...] = y


def _ffn(x, mod_l, norm_g, w_in, w_out, final_g, *, mod_row, final):
    B, S, _ = x.shape
    tile = pl.BlockSpec((None, SEQ_TILE, D_MODEL), lambda b, i: (b, i, 0))
    return pl.pallas_call(
        functools.partial(_ffn_kernel, mod_row=mod_row, final=final),
        grid=(B, S // SEQ_TILE),
        in_specs=[
            tile,
            pl.BlockSpec((None, N_MOD, D_MODEL), lambda b, i: (b, 0, 0)),
            _resident((1, D_MODEL)),
            _resident((D_MODEL, 2 * D_FF)),
            _resident((D_FF, D_MODEL)),
            _resident((1, D_MODEL)),
        ],
        out_specs=tile,
        out_shape=jax.ShapeDtypeStruct(x.shape, F32),
        compiler_params=_params("parallel", "parallel"),
        name="ffn",
    )(x, mod_l, norm_g.reshape(1, D_MODEL), w_in, w_out, final_g.reshape(1, D_MODEL))


def _mix_in_kernel(x_ref, mod_ref, g_ref, w_k_ref, w_qvt_ref, w_u_ref, w_g_ref,
                   qt_ref, k_ref, vt_ref, z_ref, gates_ref):
    x = x_ref[...]
    shift = mod_ref[3:4, :]
    scale = mod_ref[4:5, :]
    h = (_rms_norm(x, g_ref[...]) * (1 + scale) + shift).astype(BF16)
    k_ref[...] = jnp.dot(h, w_k_ref[...], preferred_element_type=F32).astype(BF16)
    nt_dims = (((1,), (1,)), ((), ()))
    qt = lax.dot_general(w_qvt_ref[:ATTN_W, :], h, nt_dims, preferred_element_type=F32)
    qt = (qt * (HEAD_DIM ** -0.5 * LOG2E)).astype(BF16)
    qt_ref[...] = qt.reshape(N_HEADS, HEAD_W, SEQ_TILE)
    vt = lax.dot_general(w_qvt_ref[ATTN_W:, :], h, nt_dims, preferred_element_type=F32)
    vt_ref[:, :HEAD_W, :] = vt.astype(BF16).reshape(N_HEADS, HEAD_W, SEQ_TILE)
    vt_ref[:, HEAD_W:, :] = jnp.ones((N_HEADS, VT_ROWS - HEAD_W, SEQ_TILE), BF16)
    a = jnp.dot(h, w_u_ref[:, :D_MODEL], preferred_element_type=F32)
    g = jnp.dot(h, w_u_ref[:, D_MODEL:], preferred_element_type=F32)
    z_ref[...] = a * jax.nn.sigmoid(g)
    gates_ref[...] = jax.nn.sigmoid(jnp.dot(h, w_g_ref[...], preferred_element_type=F32))


def _mix_in(x, mod_l, norm_g, w_k, w_qvt, w_u, w_g):
    B, S, _ = x.shape
    nt = S // SEQ_TILE
    tile = pl.BlockSpec((None, SEQ_TILE, D_MODEL), lambda b, i: (b, i, 0))
    return pl.pallas_call(
        _mix_in_kernel,
        grid=(B, nt),
        in_specs=[
            tile,
            pl.BlockSpec((None, N_MOD, D_MODEL), lambda b, i: (b, 0, 0)),
            _resident((1, D_MODEL)),
            _resident((D_MODEL, ATTN_W)),
            _resident((2 * ATTN_W, D_MODEL)),
            _resident((D_MODEL, 2 * D_MODEL)),
            _resident((D_MODEL, 2 * D_MODEL)),
        ],
        out_specs=[
            pl.BlockSpec((None, None, N_HEADS, HEAD_W, SEQ_TILE), lambda b, i: (b, i, 0, 0, 0)),
            tile,
            pl.BlockSpec((None, None, N_HEADS, VT_ROWS, SEQ_TILE), lambda b, i: (b, i, 0, 0, 0)),
            tile,
            pl.BlockSpec((None, SEQ_TILE, 2 * D_MODEL), lambda b, i: (b, i, 0)),
        ],
        out_shape=[
            jax.ShapeDtypeStruct((B, nt, N_HEADS, HEAD_W, SEQ_TILE), BF16),
            jax.ShapeDtypeStruct((B, S, ATTN_W), BF16),
            jax.ShapeDtypeStruct((B, nt, N_HEADS, VT_ROWS, SEQ_TILE), BF16),
            jax.ShapeDtypeStruct((B, S, D_MODEL), F32),
            jax.ShapeDtypeStruct((B, S, 2 * D_MODEL), F32),
        ],
        compiler_params=_params("parallel", "parallel"),
        name="mix_in",
    )(x, mod_l, norm_g.reshape(1, D_MODEL), w_k, w_qvt, w_u, w_g)


def _bias_kernel(table_ref, o_ref):
    h = pl.program_id(0)
    t = pl.program_id(1)
    shape = (KV_TILE, SEQ_TILE)
    dist = (lax.broadcasted_iota(jnp.int32, shape, 1) - lax.broadcasted_iota(jnp.int32, shape, 0)
            + (t - 1) * KV_TILE)
    n = jnp.maximum(dist, 0)
    max_exact = NUM_BUCKETS // 2
    nf = jnp.maximum(n, 1).astype(F32)
    large = max_exact + (jnp.log(nf / max_exact) / math.log(MAX_DISTANCE / max_exact)
                         * (NUM_BUCKETS - max_exact)).astype(jnp.int32)
    large = jnp.minimum(large, NUM_BUCKETS - 1)
    bucket = jnp.where(n < max_exact, n, large)
    far = table_ref[h, NUM_BUCKETS - 1]
    val = jnp.zeros(shape, F32)
    for b in range(NUM_BUCKETS - 1):
        val = jnp.where(bucket == b, (table_ref[h, b] - far) * LOG2E, val)
    o_ref[...] = jnp.where(dist >= 0, val, MASK_BIAS)


def _bias_tiles(rel_bias):
    return pl.pallas_call(
        _bias_kernel,
        grid=(N_HEADS, N_BIAS_TILES),
        in_specs=[pl.BlockSpec(memory_space=pltpu.SMEM)],
        out_specs=pl.BlockSpec((None, None, KV_TILE, SEQ_TILE), lambda h, t: (h, t, 0, 0)),
        out_shape=jax.ShapeDtypeStruct((N_HEADS, N_BIAS_TILES, KV_TILE, SEQ_TILE), F32),
        compiler_params=_params("arbitrary", "arbitrary"),
        name="rel_bias_tiles",
    )(rel_bias)


def _attn_kernel(qt_ref, k_ref, vt_ref, bias_ref, lamv_ref, sg_ref, o_ref,
                 qs_ref, s0_ref, s1_ref, mb0_ref, mb1_ref, p0_ref, p1_ref, al0_ref, al1_ref,
                 m_ref, acc_ref, *, lam_init):
    T = SEQ_TILE
    TK = KV_TILE
    per_tile = T // TK
    qi = pl.program_id(2)
    last_kb = per_tile * qi + (per_tile - 1)

    qt = qt_ref[...]
    chan = lax.broadcasted_iota(jnp.int32, qt.shape, 0)
    zero = jnp.zeros_like(qt)
    qs_ref[:, 0:T] = jnp.where(chan < HEAD_DIM, qt, zero)
    qs_ref[:, T:2 * T] = jnp.where(chan >= HEAD_DIM, qt, zero)

    m_ref[...] = jnp.full(m_ref.shape, MASK_BIAS, F32)
    acc_ref[...] = jnp.zeros(acc_ref.shape, F32)

    n_cb = 2 * T // MXU_COLS

    def scores(kb, c, s_ref, mb_ref):
        kblk = k_ref[pl.ds(pl.multiple_of(kb * TK, TK), TK), :]
        s = jnp.dot(kblk, qs_ref[:, c * MXU_COLS:(c + 1) * MXU_COLS],
                    preferred_element_type=F32)
        mx = None
        for r in range(0, TK, SUBLANES):
            chunk = s[r:r + SUBLANES, :]
            s_ref[c, r:r + SUBLANES, :] = chunk
            mx = chunk if mx is None else jnp.maximum(mx, chunk)
        mb_ref[c] = jnp.max(mx, axis=0, keepdims=True)

    def softmax(kb, c, s_ref, mb_ref, p_ref, al_ref, biased):
        s = s_ref[c]
        if biased:
            q0 = c * MXU_COLS % T
            s = s + bias_ref[jnp.clip(last_kb - kb, 0, N_BIAS_TILES - 1), :, q0:q0 + MXU_COLS]
            mb = jnp.max(s, axis=0, keepdims=True)
        else:
            mb = mb_ref[c]
        m_prev = m_ref[c]
        m_new = jnp.maximum(m_prev, mb)
        al_ref[c] = jnp.exp2(m_prev - m_new)
        m_ref[c] = m_new
        p_ref[c] = jnp.exp2(s - m_new).astype(BF16)

    def weighted_values(j, half, c, p_ref, al_ref):
        vt = vt_ref[j, :, half * TK:(half + 1) * TK]
        pv = jnp.dot(vt, p_ref[c], preferred_element_type=F32)
        acc_ref[c] = al_ref[c] * acc_ref[c] + pv

    def trip(j, biased):
        kb0 = per_tile * j
        for c in range(n_cb):
            scores(kb0 + 1, c, s1_ref, mb1_ref)
            softmax(kb0, c, s0_ref, mb0_ref, p0_ref, al0_ref, biased)
            weighted_values(j, 0, c, p0_ref, al0_ref)
        for c in range(n_cb):
            scores(jnp.minimum(kb0 + 2, last_kb), c, s0_ref, mb0_ref)
            softmax(kb0 + 1, c, s1_ref, mb1_ref, p1_ref, al1_ref, biased)
            weighted_values(j, 1, c, p1_ref, al1_ref)

    def far_trips(jj, carry):
        for u in range(FAR_UNROLL):
            trip(FAR_UNROLL * jj + u, False)
        return carry

    def far_trip(j, carry):
        trip(j, False)
        return carry

    for c in range(n_cb):
        scores(0, c, s0_ref, mb0_ref)
    n_far = jnp.maximum(qi - 1, 0)
    n_unrolled = n_far // FAR_UNROLL
    lax.fori_loop(0, n_unrolled, far_trips, 0)
    lax.fori_loop(FAR_UNROLL * n_unrolled, n_far, far_trip, 0)

    @pl.when(qi >= 1)
    def _():
        trip(qi - 1, True)
        trip(qi, True)

    @pl.when(qi == 0)
    def _():
        trip(qi, True)

    lamv = lamv_ref[...]
    lam = (jnp.exp(jnp.sum(lamv[0:1] * lamv[1:2], axis=1, keepdims=True))
           - jnp.exp(jnp.sum(lamv[2:3] * lamv[3:4], axis=1, keepdims=True)) + lam_init)

    def normalised(c):
        return acc_ref[c, :HEAD_W, :] / acc_ref[c, HEAD_W:HEAD_W + 1, :]

    half_cb = n_cb // 2
    o = jnp.concatenate([normalised(c) - lam * normalised(c + half_cb) for c in range(half_cb)],
                        axis=1)
    o = o * lax.rsqrt(jnp.mean(o * o, axis=0, keepdims=True) + EPS) * sg_ref[...]
    o = o * (1 - lam_init)
    o_ref[...] = o.T.astype(BF16)


def _attention(qt, k, vt, bias, lamv, subln_g, *, lam_init):
    B, S, _ = k.shape
    T = SEQ_TILE
    nt = S // T
    n_cb = 2 * T // MXU_COLS
    return pl.pallas_call(
        functools.partial(_attn_kernel, lam_init=lam_init),
        grid=(B, N_HEADS, nt),
        in_specs=[
            pl.BlockSpec((None, None, None, HEAD_W, T), lambda b, h, i: (b, i, h, 0, 0)),
            pl.BlockSpec((None, S, HEAD_W), lambda b, h, i: (b, 0, h)),
            pl.BlockSpec((None, nt, None, VT_ROWS, T), lambda b, h, i: (b, 0, h, 0, 0)),
            pl.BlockSpec((None, N_BIAS_TILES, KV_TILE, T), lambda b, h, i: (h, 0, 0, 0)),
            pl.BlockSpec((4, HEAD_DIM), lambda b, h, i: (0, 0)),
            pl.BlockSpec((HEAD_W, 1), lambda b, h, i: (0, 0)),
        ],
        out_specs=pl.BlockSpec((None, T, HEAD_W), lambda b, h, i: (b, i, h)),
        out_shape=jax.ShapeDtypeStruct((B, S, ATTN_W), BF16),
        scratch_shapes=[
            pltpu.VMEM((HEAD_W, 2 * T), BF16),
            pltpu.VMEM((n_cb, KV_TILE, MXU_COLS), F32),
            pltpu.VMEM((n_cb, KV_TILE, MXU_COLS), F32),
            pltpu.VMEM((n_cb, 1, MXU_COLS), F32),
            pltpu.VMEM((n_cb, 1, MXU_COLS), F32),
            pltpu.VMEM((n_cb, KV_TILE, MXU_COLS), BF16),
            pltpu.VMEM((n_cb, KV_TILE, MXU_COLS), BF16),
            pltpu.VMEM((n_cb, 1, MXU_COLS), F32),
            pltpu.VMEM((n_cb, 1, MXU_COLS), F32),
            pltpu.VMEM((n_cb, 1, MXU_COLS), F32),
            pltpu.VMEM((n_cb, VT_ROWS, MXU_COLS), F32),
        ],
        compiler_params=_params("parallel", "parallel", "arbitrary"),
        name="diff_attn",
    )(qt, k, vt, bias, lamv, subln_g.reshape(HEAD_W, 1))


def _conv_kernel(z_ref, zprev_ref, w8_ref, b_ref, lng_ref, lnb_ref, o_ref, zext_ref, y_ref):
    T = SEQ_TILE
    i = pl.program_id(1)
    prev = zprev_ref[...]
    zext_ref[0:CONV_HALO, :] = jnp.where(i == 0, jnp.zeros_like(prev), prev)
    zext_ref[CONV_HALO:, :] = z_ref[...]

    halo_groups = CONV_HALO // SUBLANES
    n_a = (CONV_K - 1) // SUBLANES + 1
    sub = lax.broadcasted_iota(jnp.int32, (SUBLANES, LANES), 0)

    for c0 in range(0, z_ref.shape[1], LANES):
        cs = slice(c0, c0 + LANES)
        w = [w8_ref[SUBLANES * j:SUBLANES * (j + 1), cs] for j in range(CONV_K)]
        bias = b_ref[:, cs]

        def partial_sums(zs):
            ys = []
            for b in range(SUBLANES):
                acc = None
                for a in range(n_a):
                    s = SUBLANES * a + b
                    if s < CONV_K:
                        t = w[CONV_K - 1 - s] * zs[a]
                        acc = t if acc is None else acc + t
                ys.append(acc)
            return ys

        def group(g):
            return zext_ref[pl.ds(pl.multiple_of(g * SUBLANES, SUBLANES), SUBLANES), cs]

        zs0 = [zext_ref[(halo_groups - 1 - a) * SUBLANES:(halo_groups - a) * SUBLANES, cs]
               for a in range(n_a)]
        ys0 = partial_sums(zs0)
        rolled0 = tuple(pltpu.roll(ys0[b], b, 0) for b in range(1, SUBLANES))

        def body(g, carry):
            rolled_prev, z_hist = carry
            zs = (group(g + halo_groups),) + z_hist
            ys = partial_sums(zs)
            out = ys[0] + bias
            rolled = []
            for b in range(1, SUBLANES):
                r = pltpu.roll(ys[b], b, 0)
                out = out + jnp.where(sub >= b, r, rolled_prev[b - 1])
                rolled.append(r)
            y_ref[pl.ds(pl.multiple_of(g * SUBLANES, SUBLANES), SUBLANES), cs] = out
            return tuple(rolled), zs[:n_a - 1]

        lax.fori_loop(0, T // SUBLANES, body, (rolled0, tuple(zs0[:n_a - 1])), unroll=2)

    rows = 32

    def ln_body(r, carry):
        rs = pl.ds(pl.multiple_of(r * rows, rows), rows)
        acc = y_ref[rs, :]
        mu = jnp.mean(acc, axis=-1, keepdims=True)
        xc = acc - mu
        y = xc * lax.rsqrt(jnp.mean(xc * xc, axis=-1, keepdims=True) + EPS)
        y = y * lng_ref[...] + lnb_ref[...]
        o_ref[rs, :] = (y * jax.nn.sigmoid(y)).astype(BF16)
        return carry

    lax.fori_loop(0, T // rows, ln_body, 0, unroll=4)


def _conv_module(z, w_dw, b_dw, ln_g, ln_b):
    B, S, C = z.shape
    T = SEQ_TILE
    halo_blocks = T // CONV_HALO
    vec = lambda a: a.reshape(1, C)
    w8 = jnp.repeat(w_dw, SUBLANES, axis=0)
    return pl.pallas_call(
        _conv_kernel,
        grid=(B, S // T),
        in_specs=[
            pl.BlockSpec((None, T, C), lambda b, i: (b, i, 0)),
            pl.BlockSpec((None, CONV_HALO, C), lambda b, i: (b, jnp.maximum(i * halo_blocks - 1, 0), 0)),
            _resident((CONV_K * SUBLANES, C)),
            _resident((1, C)),
            _resident((1, C)),
            _resident((1, C)),
        ],
        out_specs=pl.BlockSpec((None, T, C), lambda b, i: (b, i, 0)),
        out_shape=jax.ShapeDtypeStruct((B, S, C), BF16),
        scratch_shapes=[pltpu.VMEM((T + CONV_HALO, C), F32), pltpu.VMEM((T, C), F32)],
        compiler_params=_params("parallel", "parallel"),
        name="conv_module",
    )(z, z, w8, vec(b_dw), vec(ln_g), vec(ln_b))


def _mix_out_kernel(x_ref, mod_ref, o_ref, cact_ref, gates_ref, w_ab_ref, w_cb_ref, w_mo_ref, y_ref):
    y_attn = jnp.dot(o_ref[...], w_ab_ref[...], preferred_element_type=F32)
    y_conv = jnp.dot(cact_ref[...], w_cb_ref[...], preferred_element_type=F32)
    merged = gates_ref[:, :D_MODEL] * y_attn + gates_ref[:, D_MODEL:] * y_conv
    out = jnp.dot(merged.astype(BF16), w_mo_ref[...], preferred_element_type=F32)
    y_ref[...] = x_ref[...] + mod_ref[5:6, :] * out


def _mix_out(x, mod_l, o, cact, gates, w_ab, w_cb, w_mo):
    B, S, _ = x.shape
    tile = pl.BlockSpec((None, SEQ_TILE, D_MODEL), lambda b, i: (b, i, 0))
    return pl.pallas_call(
        _mix_out_kernel,
        grid=(B, S // SEQ_TILE),
        in_specs=[
            tile,
            pl.BlockSpec((None, N_MOD, D_MODEL), lambda b, i: (b, 0, 0)),
            tile,
            tile,
            pl.BlockSpec((None, SEQ_TILE, 2 * D_MODEL), lambda b, i: (b, i, 0)),
            _resident((D_MODEL, D_MODEL)),
            _resident((D_MODEL, D_MODEL)),
            _resident((D_MODEL, D_MODEL)),
        ],
        out_specs=tile,
        out_shape=jax.ShapeDtypeStruct(x.shape, F32),
        compiler_params=_params("parallel", "parallel"),
        name="mix_out",
    )(x, mod_l, o, cact, gates, w_ab, w_cb, w_mo)


def kernel(x, c, ffn1_norm, ffn1_w_in, ffn1_w_out, mix_norm, w_mix_in, lambda_q1, lambda_k1, lambda_q2, lambda_k2, subln_gain, w_attn_branch, conv_dw, conv_dw_bias, conv_ln_gain, conv_ln_bias, w_conv_branch, w_mix_out, ffn2_norm, ffn2_w_in, ffn2_w_out, ada_w, ada_b, rel_bias, final_norm):
    B, S, _ = x.shape
    assert S % SEQ_TILE == 0 and SEQ_TILE % CONV_HALO == 0 and CONV_HALO >= CONV_K - 1
    assert FF_CHUNKS[0][0] == 0 and FF_CHUNKS[-1][1] == D_FF
    assert SEQ_TILE == 2 * KV_TILE and KV_TILE >= MAX_DISTANCE

    mod_rows = 8
    c_pad = jnp.zeros((mod_rows, D_MODEL), F32).at[:B].set(c)
    mod = _modulation(c_pad, ada_w, ada_b)[:, :B].reshape(DEPTH, B, N_MOD, D_MODEL)
    bias = _bias_tiles(rel_bias)

    bf = lambda w: w.astype(BF16)
    for l in range(DEPTH):
        lam_init = 0.8 - 0.6 * math.exp(-0.3 * l)
        mod_l = mod[l]
        x = _ffn(x, mod_l, ffn1_norm[l], bf(ffn1_w_in[l]), bf(ffn1_w_out[l]), final_norm,
                 mod_row=0, final=False)

        w_mix = w_mix_in[l]
        w_qv_t = jnp.concatenate([w_mix[:, :ATTN_W], w_mix[:, 2 * ATTN_W:3 * ATTN_W]], axis=1).T
        qt, k, vt, z, gates = _mix_in(
            x, mod_l, mix_norm[l],
            bf(w_mix[:, ATTN_W:2 * ATTN_W]),
            bf(w_qv_t),
            bf(w_mix[:, 3 * ATTN_W:3 * ATTN_W + 2 * D_MODEL]),
            bf(w_mix[:, 3 * ATTN_W + 2 * D_MODEL:]))
        lamv = jnp.stack([lambda_q1[l], lambda_k1[l], lambda_q2[l], lambda_k2[l]]).astype(F32)
        o = _attention(qt, k, vt, bias, lamv, subln_gain[l], lam_init=lam_init)
        cact = _conv_module(z, conv_dw[l], conv_dw_bias[l], conv_ln_gain[l], conv_ln_bias[l])
        x = _mix_out(x, mod_l, o, cact, gates, bf(w_attn_branch[l]), bf(w_conv_branch[l]),
                     bf(w_mix_out[l]))

        x = _ffn(x, mod_l, ffn2_norm[l], bf(ffn2_w_in[l]), bf(ffn2_w_out[l]), final_norm,
                 mod_row=6, final=(l == DEPTH - 1))
    return x
```

```python
import functools
import math

import jax
import jax.numpy as jnp
from jax import lax
from jax.experimental import pallas as pl
from jax.experimental.pallas import tpu as pltpu

D_MODEL = 1024
DEPTH = 2
N_HEADS = 8
HEAD_DIM = 64
HEAD_W = 2 * HEAD_DIM
ATTN_W = N_HEADS * HEAD_W
CONV_K = 31
D_FF = 2816
NUM_BUCKETS = 32
MAX_DISTANCE = 128
N_MOD = 9
EPS = 1e-6

SUBLANES = 8
LANES = 128
MXU_COLS = 256

SEQ_TILE = 512
KV_TILE = 256
FAR_UNROLL = 4
N_BIAS_TILES = 4
LOG2E = math.log2(math.e)
BF16_SUBLANES = 16
VT_ROWS = HEAD_W + BF16_SUBLANES
CONV_HALO = 32
FF_CHUNKS = ((0, 5 * MXU_COLS), (5 * MXU_COLS, D_FF))
CAST_BLOCK_BYTES = 6 * 1024 * 1024
MASK_BIAS = -1e30
VMEM_LIMIT = 56 * 1024 * 1024

F32 = jnp.float32
BF16 = jnp.bfloat16


def _params(*sem):
    return pltpu.CompilerParams(dimension_semantics=sem, vmem_limit_bytes=VMEM_LIMIT)


def _resident(shape, index=None):
    index = (0,) * len(shape) if index is None else index
    return pl.BlockSpec(shape, lambda *_: index, pipeline_mode=pl.Buffered(1))


def _cast_kernel(x_ref, o_ref):
    o_ref[...] = x_ref[...].astype(BF16)


def _to_bf16(w):
    L, R, C = w.shape
    rb = R
    while rb * C * 4 > CAST_BLOCK_BYTES and rb % (2 * BF16_SUBLANES) == 0:
        rb //= 2
    spec = pl.BlockSpec((None, rb, C), lambda l, i: (l, i, 0))
    return pl.pallas_call(
        _cast_kernel,
        grid=(L, R // rb),
        in_specs=[spec],
        out_specs=spec,
        out_shape=jax.ShapeDtypeStruct(w.shape, BF16),
        compiler_params=_params("parallel", "parallel"),
        name="to_bf16",
    )(w)


def _rms_norm(x, g):
    return x * lax.rsqrt(jnp.mean(x * x, axis=-1, keepdims=True) + EPS) * g


def _mod_kernel(c_ref, w_ref, b_ref, o_ref):
    c = c_ref[...]
    c_act = (c * jax.nn.sigmoid(c)).astype(BF16)
    o_ref[...] = jnp.dot(c_act, w_ref[...].astype(BF16), preferred_element_type=F32) + b_ref[...]


def _modulation(c_pad, ada_w, ada_b):
    rows = c_pad.shape[0]
    return pl.pallas_call(
        _mod_kernel,
        grid=(DEPTH, N_MOD),
        in_specs=[
            pl.BlockSpec((rows, D_MODEL), lambda l, j: (0, 0)),
            pl.BlockSpec((None, D_MODEL, D_MODEL), lambda l, j: (l, 0, j)),
            pl.BlockSpec((None, 1, D_MODEL), lambda l, j: (l, 0, j)),
        ],
        out_specs=pl.BlockSpec((None, rows, D_MODEL), lambda l, j: (l, 0, j)),
        out_shape=jax.ShapeDtypeStruct((DEPTH, rows, N_MOD * D_MODEL), F32),
        compiler_params=_params("arbitrary", "arbitrary"),
        name="adaln_mod",
    )(c_pad, ada_w, ada_b.reshape(DEPTH, 1, N_MOD * D_MODEL))


def _ffn_kernel(x_ref, mod_ref, g_ref, w_in_ref, w_out_ref, fg_ref, o_ref, *, mod_row, final):
    x = x_ref[...]
    shift = mod_ref[mod_row:mod_row + 1, :]
    scale = mod_ref[mod_row + 1:mod_row + 2, :]
    gate = mod_ref[mod_row + 2:mod_row + 3, :]
    h = (_rms_norm(x, g_ref[...]) * (1 + scale) + shift).astype(BF16)
    out = None
    for c0, c1 in FF_CHUNKS:
        g = jnp.dot(h, w_in_ref[:, c0:c1], preferred_element_type=F32)
        u = jnp.dot(h, w_in_ref[:, D_FF + c0:D_FF + c1], preferred_element_type=F32)
        act = (g * jax.nn.sigmoid(g) * u).astype(BF16)
        part = jnp.dot(act, w_out_ref[c0:c1, :], preferred_element_type=F32)
        out = part if out is None else out + part
    y = x + (0.5 * gate) * out
    if final:
        y = _rms_norm(y, fg_ref[...])
    o_ref[...] = y


def _ffn(x, mod_l, norm_g, w_in, w_out, final_g, *, layer, mod_row, final):
    B, S, _ = x.shape
    tile = pl.BlockSpec((None, SEQ_TILE, D_MODEL), lambda b, i: (b, i, 0))
    return pl.pallas_call(
        functools.partial(_ffn_kernel, mod_row=mod_row, final=final),
        grid=(B, S // SEQ_TILE),
        in_specs=[
            tile,
            pl.BlockSpec((None, N_MOD, D_MODEL), lambda b, i: (b, 0, 0)),
            _resident((1, D_MODEL)),
            _resident((None, D_MODEL, 2 * D_FF), (layer, 0, 0)),
            _resident((None, D_FF, D_MODEL), (layer, 0, 0)),
            _resident((1, D_MODEL)),
        ],
        out_specs=tile,
        out_shape=jax.ShapeDtypeStruct(x.shape, F32),
        compiler_params=_params("parallel", "parallel"),
        name="ffn",
    )(x, mod_l, norm_g.reshape(1, D_MODEL), w_in, w_out, final_g.reshape(1, D_MODEL))


def _mix_in_kernel(x_ref, mod_ref, g_ref, w_q_ref, w_k_ref, w_vt_ref, w_a_ref, w_g_ref, w_ga_ref, w_gb_ref,
                   q_ref, k_ref, vt_ref, z_ref, gates_ref):
    x = x_ref[...]
    shift = mod_ref[3:4, :]
    scale = mod_ref[4:5, :]
    h = (_rms_norm(x, g_ref[...]) * (1 + scale) + shift).astype(BF16)
    q = jnp.dot(h, w_q_ref[...], preferred_element_type=F32)
    q_ref[...] = (q * (HEAD_DIM ** -0.5 * LOG2E)).astype(BF16)
    k_ref[...] = jnp.dot(h, w_k_ref[...], preferred_element_type=F32).astype(BF16)
    vt = lax.dot_general(w_vt_ref[...], h, (((1,), (1,)), ((), ())), preferred_element_type=F32)
    vt_ref[:, :HEAD_W, :] = vt.astype(BF16).reshape(N_HEADS, HEAD_W, SEQ_TILE)
    vt_ref[:, HEAD_W:, :] = jnp.ones((N_HEADS, VT_ROWS - HEAD_W, SEQ_TILE), BF16)
    a = jnp.dot(h, w_a_ref[...], preferred_element_type=F32)
    g = jnp.dot(h, w_g_ref[...], preferred_element_type=F32)
    z_ref[...] = a * jax.nn.sigmoid(g)
    gates_ref[:, :D_MODEL] = jax.nn.sigmoid(jnp.dot(h, w_ga_ref[...], preferred_element_type=F32))
    gates_ref[:, D_MODEL:] = jax.nn.sigmoid(jnp.dot(h, w_gb_ref[...], preferred_element_type=F32))


def _mix_in(x, mod_l, norm_g, w_mix, w_vt, *, layer):
    B, S, _ = x.shape
    nt = S // SEQ_TILE
    tile = pl.BlockSpec((None, SEQ_TILE, D_MODEL), lambda b, i: (b, i, 0))
    col_block = lambda j: _resident((None, D_MODEL, D_MODEL), (layer, 0, j))
    return pl.pallas_call(
        _mix_in_kernel,
        grid=(B, nt),
        in_specs=[
            tile,
            pl.BlockSpec((None, N_MOD, D_MODEL), lambda b, i: (b, 0, 0)),
            _resident((1, D_MODEL)),
            col_block(0),
            col_block(1),
            _resident((ATTN_W, D_MODEL)),
            col_block(3),
            col_block(4),
            col_block(5),
            col_block(6),
        ],
        out_specs=[
            tile,
            tile,
            pl.BlockSpec((None, None, N_HEADS, VT_ROWS, SEQ_TILE), lambda b, i: (b, i, 0, 0, 0)),
            tile,
            pl.BlockSpec((None, SEQ_TILE, 2 * D_MODEL), lambda b, i: (b, i, 0)),
        ],
        out_shape=[
            jax.ShapeDtypeStruct((B, S, ATTN_W), BF16),
            jax.ShapeDtypeStruct((B, S, ATTN_W), BF16),
            jax.ShapeDtypeStruct((B, nt, N_HEADS, VT_ROWS, SEQ_TILE), BF16),
            jax.ShapeDtypeStruct((B, S, D_MODEL), F32),
            jax.ShapeDtypeStruct((B, S, 2 * D_MODEL), F32),
        ],
        compiler_params=_params("parallel", "parallel"),
        name="mix_in",
    )(x, mod_l, norm_g.reshape(1, D_MODEL), w_mix, w_mix, w_vt, w_mix, w_mix, w_mix, w_mix)


def _bias_kernel(table_ref, o_ref):
    h = pl.program_id(0)
    t = pl.program_id(1)
    shape = (KV_TILE, SEQ_TILE)
    dist = (lax.broadcasted_iota(jnp.int32, shape, 1) - lax.broadcasted_iota(jnp.int32, shape, 0)
            + (t - 1) * KV_TILE)
    n = jnp.maximum(dist, 0)
    max_exact = NUM_BUCKETS // 2
    nf = jnp.maximum(n, 1).astype(F32)
    large = max_exact + (jnp.log(nf / max_exact) / math.log(MAX_DISTANCE / max_exact)
                         * (NUM_BUCKETS - max_exact)).astype(jnp.int32)
    large = jnp.minimum(large, NUM_BUCKETS - 1)
    bucket = jnp.where(n < max_exact, n, large)
    far = table_ref[h, NUM_BUCKETS - 1]
    val = jnp.zeros(shape, F32)
    for b in range(NUM_BUCKETS - 1):
        val = jnp.where(bucket == b, (table_ref[h, b] - far) * LOG2E, val)
    o_ref[...] = jnp.where(dist >= 0, val, MASK_BIAS)


def _bias_tiles(rel_bias):
    return pl.pallas_call(
        _bias_kernel,
        grid=(N_HEADS, N_BIAS_TILES),
        in_specs=[pl.BlockSpec(memory_space=pltpu.SMEM)],
        out_specs=pl.BlockSpec((None, None, KV_TILE, SEQ_TILE), lambda h, t: (h, t, 0, 0)),
        out_shape=jax.ShapeDtypeStruct((N_HEADS, N_BIAS_TILES, KV_TILE, SEQ_TILE), F32),
        compiler_params=_params("arbitrary", "arbitrary"),
        name="rel_bias_tiles",
    )(rel_bias)


def _attn_kernel(q_ref, k_ref, vt_ref, bias_ref, lamv_ref, sg_ref, o_ref,
                 qs_ref, s0_ref, s1_ref, mb0_ref, mb1_ref, p0_ref, p1_ref, al0_ref, al1_ref,
                 m_ref, acc_ref, *, lam_init):
    T = SEQ_TILE
    TK = KV_TILE
    per_tile = T // TK
    qi = pl.program_id(2)
    last_kb = per_tile * qi + (per_tile - 1)

    q = q_ref[...]
    lane = lax.broadcasted_iota(jnp.int32, q.shape, 1)
    zero = jnp.zeros_like(q)
    qs_ref[0:T, :] = jnp.where(lane < HEAD_DIM, q, zero)
    qs_ref[T:2 * T, :] = jnp.where(lane >= HEAD_DIM, q, zero)

    m_ref[...] = jnp.full(m_ref.shape, MASK_BIAS, F32)
    acc_ref[...] = jnp.zeros(acc_ref.shape, F32)

    n_cb = 2 * T // MXU_COLS

    def scores(kb, c, s_ref, mb_ref):
        kblk = k_ref[pl.ds(pl.multiple_of(kb * TK, TK), TK), :]
        s = lax.dot_general(kblk, qs_ref[c * MXU_COLS:(c + 1) * MXU_COLS, :], (((1,), (1,)), ((), ())),
                            preferred_element_type=F32)
        mx = None
        for r in range(0, TK, SUBLANES):
            chunk = s[r:r + SUBLANES, :]
            s_ref[c, r:r + SUBLANES, :] = chunk
            mx = chunk if mx is None else jnp.maximum(mx, chunk)
        mb_ref[c] = jnp.max(mx, axis=0, keepdims=True)

    def softmax(kb, c, s_ref, mb_ref, p_ref, al_ref, biased):
        s = s_ref[c]
        if biased:
            q0 = c * MXU_COLS % T
            s = s + bias_ref[jnp.clip(last_kb - kb, 0, N_BIAS_TILES - 1), :, q0:q0 + MXU_COLS]
            mb = jnp.max(s, axis=0, keepdims=True)
        else:
            mb = mb_ref[c]
        m_prev = m_ref[c]
        m_new = jnp.maximum(m_prev, mb)
        al_ref[c] = jnp.exp2(m_prev - m_new)
        m_ref[c] = m_new
        p_ref[c] = jnp.exp2(s - m_new).astype(BF16)

    def weighted_values(j, half, c, p_ref, al_ref):
        vt = vt_ref[j, :, half * TK:(half + 1) * TK]
        pv = jnp.dot(vt, p_ref[c], preferred_element_type=F32)
        acc_ref[c] = al_ref[c] * acc_ref[c] + pv

    def trip(j, biased):
        kb0 = per_tile * j
        for c in range(n_cb):
            scores(kb0 + 1, c, s1_ref, mb1_ref)
            softmax(kb0, c, s0_ref, mb0_ref, p0_ref, al0_ref, biased)
            weighted_values(j, 0, c, p0_ref, al0_ref)
        for c in range(n_cb):
            scores(jnp.minimum(kb0 + 2, last_kb), c, s0_ref, mb0_ref)
            softmax(kb0 + 1, c, s1_ref, mb1_ref, p1_ref, al1_ref, biased)
            weighted_values(j, 1, c, p1_ref, al1_ref)

    def far_trips(jj, carry):
        for u in range(FAR_UNROLL):
            trip(FAR_UNROLL * jj + u, False)
        return carry

    def far_trip(j, carry):
        trip(j, False)
        return carry

    for c in range(n_cb):
        scores(0, c, s0_ref, mb0_ref)
    n_far = jnp.maximum(qi - 1, 0)
    n_unrolled = n_far // FAR_UNROLL
    lax.fori_loop(0, n_unrolled, far_trips, 0)
    lax.fori_loop(FAR_UNROLL * n_unrolled, n_far, far_trip, 0)

    @pl.when(qi >= 1)
    def _():
        trip(qi - 1, True)
        trip(qi, True)

    @pl.when(qi == 0)
    def _():
        trip(qi, True)

    lamv = lamv_ref[...]
    lam = (jnp.exp(jnp.sum(lamv[0:1] * lamv[1:2], axis=1, keepdims=True))
           - jnp.exp(jnp.sum(lamv[2:3] * lamv[3:4], axis=1, keepdims=True)) + lam_init)

    def normalised(c):
        return acc_ref[c, :HEAD_W, :] / acc_ref[c, HEAD_W:HEAD_W + 1, :]

    half_cb = n_cb // 2
    o = jnp.concatenate([normalised(c) - lam * normalised(c + half_cb) for c in range(half_cb)],
                        axis=1)
    o = o * lax.rsqrt(jnp.mean(o * o, axis=0, keepdims=True) + EPS) * sg_ref[...]
    o = o * (1 - lam_init)
    o_ref[...] = o.T.astype(BF16)


def _attention(q, k, vt, bias, lamv, subln_g, *, lam_init):
    B, S, _ = q.shape
    T = SEQ_TILE
    nt = S // T
    n_cb = 2 * T // MXU_COLS
    return pl.pallas_call(
        functools.partial(_attn_kernel, lam_init=lam_init),
        grid=(B, N_HEADS, nt),
        in_specs=[
            pl.BlockSpec((None, T, HEAD_W), lambda b, h, i: (b, i, h)),
            pl.BlockSpec((None, S, HEAD_W), lambda b, h, i: (b, 0, h)),
            pl.BlockSpec((None, nt, None, VT_ROWS, T), lambda b, h, i: (b, 0, h, 0, 0)),
            pl.BlockSpec((None, N_BIAS_TILES, KV_TILE, T), lambda b, h, i: (h, 0, 0, 0)),
            pl.BlockSpec((4, HEAD_DIM), lambda b, h, i: (0, 0)),
            pl.BlockSpec((HEAD_W, 1), lambda b, h, i: (0, 0)),
        ],
        out_specs=pl.BlockSpec((None, T, HEAD_W), lambda b, h, i: (b, i, h)),
        out_shape=jax.ShapeDtypeStruct((B, S, ATTN_W), BF16),
        scratch_shapes=[
            pltpu.VMEM((2 * T, HEAD_W), BF16),
            pltpu.VMEM((n_cb, KV_TILE, MXU_COLS), F32),
            pltpu.VMEM((n_cb, KV_TILE, MXU_COLS), F32),
            pltpu.VMEM((n_cb, 1, MXU_COLS), F32),
            pltpu.VMEM((n_cb, 1, MXU_COLS), F32),
            pltpu.VMEM((n_cb, KV_TILE, MXU_COLS), BF16),
            pltpu.VMEM((n_cb, KV_TILE, MXU_COLS), BF16),
            pltpu.VMEM((n_cb, 1, MXU_COLS), F32),
            pltpu.VMEM((n_cb, 1, MXU_COLS), F32),
            pltpu.VMEM((n_cb, 1, MXU_COLS), F32),
            pltpu.VMEM((n_cb, VT_ROWS, MXU_COLS), F32),
        ],
        compiler_params=_params("parallel", "parallel", "arbitrary"),
        name="diff_attn",
    )(q, k, vt, bias, lamv, subln_g.reshape(HEAD_W, 1))


def _conv_kernel(z_ref, zprev_ref, w8_ref, b_ref, lng_ref, lnb_ref, o_ref, zext_ref, y_ref):
    T = SEQ_TILE
    i = pl.program_id(1)
    prev = zprev_ref[...]
    zext_ref[0:CONV_HALO, :] = jnp.where(i == 0, jnp.zeros_like(prev), prev)
    zext_ref[CONV_HALO:, :] = z_ref[...]

    halo_groups = CONV_HALO // SUBLANES
    n_a = (CONV_K - 1) // SUBLANES + 1
    sub = lax.broadcasted_iota(jnp.int32, (SUBLANES, LANES), 0)

    for c0 in range(0, z_ref.shape[1], LANES):
        cs = slice(c0, c0 + LANES)
        w = [w8_ref[SUBLANES * j:SUBLANES * (j + 1), cs] for j in range(CONV_K)]
        bias = b_ref[:, cs]

        def partial_sums(zs):
            ys = []
            for b in range(SUBLANES):
                acc = None
                for a in range(n_a):
                    s = SUBLANES * a + b
                    if s < CONV_K:
                        t = w[CONV_K - 1 - s] * zs[a]
                        acc = t if acc is None else acc + t
                ys.append(acc)
            return ys

        def group(g):
            return zext_ref[pl.ds(pl.multiple_of(g * SUBLANES, SUBLANES), SUBLANES), cs]

        zs0 = [zext_ref[(halo_groups - 1 - a) * SUBLANES:(halo_groups - a) * SUBLANES, cs]
               for a in range(n_a)]
        ys0 = partial_sums(zs0)
        rolled0 = tuple(pltpu.roll(ys0[b], b, 0) for b in range(1, SUBLANES))

        def body(g, carry):
            rolled_prev, z_hist = carry
            zs = (group(g + halo_groups),) + z_hist
            ys = partial_sums(zs)
            out = ys[0] + bias
            rolled = []
            for b in range(1, SUBLANES):
                r = pltpu.roll(ys[b], b, 0)
                out = out + jnp.where(sub >= b, r, rolled_prev[b - 1])
                rolled.append(r)
            y_ref[pl.ds(pl.multiple_of(g * SUBLANES, SUBLANES), SUBLANES), cs] = out
            return tuple(rolled), zs[:n_a - 1]

        lax.fori_loop(0, T // SUBLANES, body, (rolled0, tuple(zs0[:n_a - 1])), unroll=2)

    rows = 32

    def ln_body(r, carry):
        rs = pl.ds(pl.multiple_of(r * rows, rows), rows)
        acc = y_ref[rs, :]
        mu = jnp.mean(acc, axis=-1, keepdims=True)
        xc = acc - mu
        y = xc * lax.rsqrt(jnp.mean(xc * xc, axis=-1, keepdims=True) + EPS)
        y = y * lng_ref[...] + lnb_ref[...]
        o_ref[rs, :] = (y * jax.nn.sigmoid(y)).astype(BF16)
        return carry

    lax.fori_loop(0, T // rows, ln_body, 0, unroll=4)


def _conv_module(z, w_dw, b_dw, ln_g, ln_b):
    B, S, C = z.shape
    T = SEQ_TILE
    halo_blocks = T // CONV_HALO
    vec = lambda a: a.reshape(1, C)
    w8 = jnp.repeat(w_dw, SUBLANES, axis=0)
    return pl.pallas_call(
        _conv_kernel,
        grid=(B, S // T),
        in_specs=[
            pl.BlockSpec((None, T, C), lambda b, i: (b, i, 0)),
            pl.BlockSpec((None, CONV_HALO, C), lambda b, i: (b, jnp.maximum(i * halo_blocks - 1, 0), 0)),
            _resident((CONV_K * SUBLANES, C)),
            _resident((1, C)),
            _resident((1, C)),
            _resident((1, C)),
        ],
        out_specs=pl.BlockSpec((None, T, C), lambda b, i: (b, i, 0)),
        out_shape=jax.ShapeDtypeStruct((B, S, C), BF16),
        scratch_shapes=[pltpu.VMEM((T + CONV_HALO, C), F32), pltpu.VMEM((T, C), F32)],
        compiler_params=_params("parallel", "parallel"),
        name="conv_module",
    )(z, z, w8, vec(b_dw), vec(ln_g), vec(ln_b))


def _mix_out_kernel(x_ref, mod_ref, o_ref, cact_ref, gates_ref, w_ab_ref, w_cb_ref, w_mo_ref, y_ref):
    y_attn = jnp.dot(o_ref[...], w_ab_ref[...], preferred_element_type=F32)
    y_conv = jnp.dot(cact_ref[...], w_cb_ref[...], preferred_element_type=F32)
    merged = gates_ref[:, :D_MODEL] * y_attn + gates_ref[:, D_MODEL:] * y_conv
    out = jnp.dot(merged.astype(BF16), w_mo_ref[...], preferred_element_type=F32)
    y_ref[...] = x_ref[...] + mod_ref[5:6, :] * out


def _mix_out(x, mod_l, o, cact, gates, w_ab, w_cb, w_mo, *, layer):
    B, S, _ = x.shape
    tile = pl.BlockSpec((None, SEQ_TILE, D_MODEL), lambda b, i: (b, i, 0))
    weight = _resident((None, D_MODEL, D_MODEL), (layer, 0, 0))
    return pl.pallas_call(
        _mix_out_kernel,
        grid=(B, S // SEQ_TILE),
        in_specs=[
            tile,
            pl.BlockSpec((None, N_MOD, D_MODEL), lambda b, i: (b, 0, 0)),
            tile,
            tile,
            pl.BlockSpec((None, SEQ_TILE, 2 * D_MODEL), lambda b, i: (b, i, 0)),
            weight,
            weight,
            weight,
        ],
        out_specs=tile,
        out_shape=jax.ShapeDtypeStruct(x.shape, F32),
        compiler_params=_params("parallel", "parallel"),
        name="mix_out",
    )(x, mod_l, o, cact, gates, w_ab, w_cb, w_mo)


def kernel(x, c, ffn1_norm, ffn1_w_in, ffn1_w_out, mix_norm, w_mix_in, lambda_q1, lambda_k1, lambda_q2, lambda_k2, subln_gain, w_attn_branch, conv_dw, conv_dw_bias, conv_ln_gain, conv_ln_bias, w_conv_branch, w_mix_out, ffn2_norm, ffn2_w_in, ffn2_w_out, ada_w, ada_b, rel_bias, final_norm):
    B, S, _ = x.shape
    assert S % SEQ_TILE == 0 and SEQ_TILE % CONV_HALO == 0 and CONV_HALO >= CONV_K - 1
    assert FF_CHUNKS[0][0] == 0 and FF_CHUNKS[-1][1] == D_FF
    assert SEQ_TILE == 2 * KV_TILE and KV_TILE >= MAX_DISTANCE

    mod_rows = 8
    c_pad = jnp.zeros((mod_rows, D_MODEL), F32).at[:B].set(c)
    mod = _modulation(c_pad, ada_w, ada_b)[:, :B].reshape(DEPTH, B, N_MOD, D_MODEL)
    bias = _bias_tiles(rel_bias)

    ffn1_in, ffn1_out = _to_bf16(ffn1_w_in), _to_bf16(ffn1_w_out)
    ffn2_in, ffn2_out = _to_bf16(ffn2_w_in), _to_bf16(ffn2_w_out)
    w_mix = _to_bf16(w_mix_in)
    w_ab, w_cb, w_mo = _to_bf16(w_attn_branch), _to_bf16(w_conv_branch), _to_bf16(w_mix_out)
    w_vt = jnp.swapaxes(w_mix[:, :, 2 * ATTN_W:3 * ATTN_W], 1, 2)

    for l in range(DEPTH):
        lam_init = 0.8 - 0.6 * math.exp(-0.3 * l)
        mod_l = mod[l]
        x = _ffn(x, mod_l, ffn1_norm[l], ffn1_in, ffn1_out, final_norm,
                 layer=l, mod_row=0, final=False)

        q, k, vt, z, gates = _mix_in(x, mod_l, mix_norm[l], w_mix, w_vt[l], layer=l)
        lamv = jnp.stack([lambda_q1[l], lambda_k1[l], lambda_q2[l], lambda_k2[l]]).astype(F32)
        o = _attention(q, k, vt, bias, lamv, subln_gain[l], lam_init=lam_init)
        cact = _conv_module(z, conv_dw[l], conv_dw_bias[l], conv_ln_gain[l], conv_ln_bias[l])
        x = _mix_out(x, mod_l, o, cact, gates, w_ab, w_cb, w_mo, layer=l)

        x = _ffn(x, mod_l, ffn2_norm[l], ffn2_in, ffn2_out, final_norm,
                 layer=l, mod_row=6, final=(l == DEPTH - 1))
    return x
```

```python
import functools
import math

import jax
import jax.numpy as jnp
from jax import lax
from jax.experimental import pallas as pl
from jax.experimental.pallas import tpu as pltpu

D_MODEL = 1024
DEPTH = 2
N_HEADS = 8
HEAD_DIM = 64
HEAD_W = 2 * HEAD_DIM
ATTN_W = N_HEADS * HEAD_W
CONV_K = 31
D_FF = 2816
NUM_BUCKETS = 32
MAX_DISTANCE = 128
N_MOD = 9
EPS = 1e-6

SUBLANES = 8
LANES = 128
MXU_COLS = 256

SEQ_TILE = 512
KV_TILE = 256
FAR_UNROLL = 4
N_BIAS_TILES = 4
LOG2E = math.log2(math.e)
BF16_SUBLANES = 16
VT_ROWS = HEAD_W + BF16_SUBLANES
CONV_HALO = 32
FF_CHUNKS = ((0, 5 * MXU_COLS), (5 * MXU_COLS, D_FF))
CAST_BLOCK_BYTES = 6 * 1024 * 1024
MASK_BIAS = -1e30
VMEM_LIMIT = 56 * 1024 * 1024

F32 = jnp.float32
BF16 = jnp.bfloat16


def _params(*sem):
    return pltpu.CompilerParams(dimension_semantics=sem, vmem_limit_bytes=VMEM_LIMIT)


def _resident(shape, index=None):
    index = (0,) * len(shape) if index is None else index
    return pl.BlockSpec(shape, lambda *_: index, pipeline_mode=pl.Buffered(1))


def _cast_kernel(x_ref, o_ref):
    o_ref[...] = x_ref[...].astype(BF16)


def _to_bf16(w):
    L, R, C = w.shape
    rb = R
    while rb * C * 4 > CAST_BLOCK_BYTES and rb % (2 * BF16_SUBLANES) == 0:
        rb //= 2
    spec = pl.BlockSpec((None, rb, C), lambda l, i: (l, i, 0))
    return pl.pallas_call(
        _cast_kernel,
        grid=(L, R // rb),
        in_specs=[spec],
        out_specs=spec,
        out_shape=jax.ShapeDtypeStruct(w.shape, BF16),
        compiler_params=_params("parallel", "parallel"),
        name="to_bf16",
    )(w)


def _rms_norm(x, g):
    return x * lax.rsqrt(jnp.mean(x * x, axis=-1, keepdims=True) + EPS) * g


def _mod_kernel(c_ref, w_ref, b_ref, o_ref):
    c = c_ref[...]
    c_act = (c * jax.nn.sigmoid(c)).astype(BF16)
    o_ref[...] = jnp.dot(c_act, w_ref[...].astype(BF16), preferred_element_type=F32) + b_ref[...]


def _modulation(c_pad, ada_w, ada_b):
    rows = c_pad.shape[0]
    return pl.pallas_call(
        _mod_kernel,
        grid=(DEPTH, N_MOD),
        in_specs=[
            pl.BlockSpec((rows, D_MODEL), lambda l, j: (0, 0)),
            pl.BlockSpec((None, D_MODEL, D_MODEL), lambda l, j: (l, 0, j)),
            pl.BlockSpec((None, 1, D_MODEL), lambda l, j: (l, 0, j)),
        ],
        out_specs=pl.BlockSpec((None, rows, D_MODEL), lambda l, j: (l, 0, j)),
        out_shape=jax.ShapeDtypeStruct((DEPTH, rows, N_MOD * D_MODEL), F32),
        compiler_params=_params("arbitrary", "arbitrary"),
        name="adaln_mod",
    )(c_pad, ada_w, ada_b.reshape(DEPTH, 1, N_MOD * D_MODEL))


def _ffn_kernel(x_ref, mod_ref, g_ref, w_in_ref, w_out_ref, fg_ref, o_ref, *, mod_row, final):
    x = x_ref[...]
    shift = mod_ref[mod_row:mod_row + 1, :]
    scale = mod_ref[mod_row + 1:mod_row + 2, :]
    gate = mod_ref[mod_row + 2:mod_row + 3, :]
    h = (_rms_norm(x, g_ref[...]) * (1 + scale) + shift).astype(BF16)
    out = None
    for c0, c1 in FF_CHUNKS:
        g = jnp.dot(h, w_in_ref[:, c0:c1], preferred_element_type=F32)
        u = jnp.dot(h, w_in_ref[:, D_FF + c0:D_FF + c1], preferred_element_type=F32)
        act = (g * jax.nn.sigmoid(g) * u).astype(BF16)
        part = jnp.dot(act, w_out_ref[c0:c1, :], preferred_element_type=F32)
        out = part if out is None else out + part
    y = x + (0.5 * gate) * out
    if final:
        y = _rms_norm(y, fg_ref[...])
    o_ref[...] = y


def _ffn(x, mod_l, norm_g, w_in, w_out, final_g, *, layer, mod_row, final):
    B, S, _ = x.shape
    tile = pl.BlockSpec((None, SEQ_TILE, D_MODEL), lambda b, i: (b, i, 0))
    return pl.pallas_call(
        functools.partial(_ffn_kernel, mod_row=mod_row, final=final),
        grid=(B, S // SEQ_TILE),
        in_specs=[
            tile,
            pl.BlockSpec((None, N_MOD, D_MODEL), lambda b, i: (b, 0, 0)),
            _resident((1, D_MODEL)),
            _resident((None, D_MODEL, 2 * D_FF), (layer, 0, 0)),
            _resident((None, D_FF, D_MODEL), (layer, 0, 0)),
            _resident((1, D_MODEL)),
        ],
        out_specs=tile,
        out_shape=jax.ShapeDtypeStruct(x.shape, F32),
        compiler_params=_params("parallel", "parallel"),
        name="ffn",
    )(x, mod_l, norm_g.reshape(1, D_MODEL), w_in, w_out, final_g.reshape(1, D_MODEL))


def _zero_from(x):
    bits = lax.bitcast_convert_type(x, jnp.uint32)
    return lax.bitcast_convert_type((bits >> 16) >> 16, F32)


def _conv_units(zext_ref, w8_ref, b_ref, y_ref, order_after):
    T, C = y_ref.shape
    halo_groups = CONV_HALO // SUBLANES
    n_a = (CONV_K - 1) // SUBLANES + 1
    sub = lax.broadcasted_iota(jnp.int32, (SUBLANES, LANES), 0)
    lane_blocks = [slice(c0, c0 + LANES) for c0 in range(0, C, LANES)]

    def partial_sums(g, cs):
        zs = [zext_ref[(g - a) * SUBLANES:(g - a + 1) * SUBLANES, cs] for a in range(n_a)]
        ys = []
        for b in range(SUBLANES):
            acc = None
            for a in range(n_a):
                s = SUBLANES * a + b
                if s < CONV_K:
                    j = CONV_K - 1 - s
                    t = w8_ref[SUBLANES * j:SUBLANES * (j + 1), cs] * zs[a]
                    acc = t if acc is None else acc + t
            ys.append(acc)
        return ys

    rolled_prev = {}
    for cs in lane_blocks:
        ys = partial_sums(halo_groups - 1, cs)
        rolled_prev[cs.start] = [pltpu.roll(ys[b], b, 0) for b in range(1, SUBLANES)]

    zero, zero_uses = None, 0
    for g in range(T // SUBLANES):
        for cs in lane_blocks:
            if order_after:
                zero, zero_uses = _zero_from(order_after.pop()), len(lane_blocks)
                while order_after:
                    zero = zero + _zero_from(order_after.pop())
            ys = partial_sums(g + halo_groups, cs)
            out = ys[0] + b_ref[:, cs]
            rolled = [pltpu.roll(ys[b], b, 0) for b in range(1, SUBLANES)]
            for b in range(1, SUBLANES):
                out = out + jnp.where(sub >= b, rolled[b - 1], rolled_prev[cs.start][b - 1])
            rolled_prev[cs.start] = rolled
            if zero_uses:
                out, zero_uses = out + zero, zero_uses - 1
            y_ref[g * SUBLANES:(g + 1) * SUBLANES, cs] = out
            yield


def _mix_in_kernel(x_ref, mod_ref, g_ref, w_q_ref, w_k_ref, w_vt_ref, w_a_ref, w_g_ref, w_ga_ref, w_gb_ref,
                   w8_ref, cb_ref, lng_ref, lnb_ref,
                   q_ref, k_ref, vt_ref, cact_ref, gates_ref,
                   zext_ref, halo_ref, y_ref):
    T = SEQ_TILE
    NB = MXU_COLS
    n_chunks = D_MODEL // NB
    x = x_ref[...]
    shift = mod_ref[3:4, :]
    scale = mod_ref[4:5, :]
    h = (_rms_norm(x, g_ref[...]) * (1 + scale) + shift).astype(BF16)

    @pl.when(pl.program_id(1) == 0)
    def _():
        zext_ref[0:CONV_HALO, :] = jnp.zeros((CONV_HALO, D_MODEL), F32)

    @pl.when(pl.program_id(1) > 0)
    def _():
        zext_ref[0:CONV_HALO, :] = halo_ref[...]

    order_after = []
    conv = _conv_units(zext_ref, w8_ref, cb_ref, y_ref, order_after)

    def run_conv(n_steps):
        for _ in range(n_steps):
            next(conv, None)

    def glu_chunk(j):
        cs = slice(j * NB, (j + 1) * NB)
        a = jnp.dot(h, w_a_ref[:, cs], preferred_element_type=F32)
        g = jnp.dot(h, w_g_ref[:, cs], preferred_element_type=F32)
        zext_ref[CONV_HALO:, cs] = a * jax.nn.sigmoid(g)
        halo_ref[:, cs] = zext_ref[T:T + CONV_HALO, cs]
        return g[:SUBLANES, :LANES]

    def q_chunk(j):
        cs = slice(j * NB, (j + 1) * NB)
        q = jnp.dot(h, w_q_ref[:, cs], preferred_element_type=F32)
        q_ref[:, cs] = (q * (HEAD_DIM ** -0.5 * LOG2E)).astype(BF16)
        return q[:SUBLANES, :LANES]

    def k_chunk(j):
        cs = slice(j * NB, (j + 1) * NB)
        k = jnp.dot(h, w_k_ref[:, cs], preferred_element_type=F32)
        k_ref[:, cs] = k.astype(BF16)
        return k[:SUBLANES, :LANES]

    def vt_chunk(j):
        heads = NB // HEAD_W
        vt = lax.dot_general(w_vt_ref[j * NB:(j + 1) * NB, :], h, (((1,), (1,)), ((), ())),
                             preferred_element_type=F32)
        vt_ref[j * heads:(j + 1) * heads, :HEAD_W, :] = vt.astype(BF16).reshape(heads, HEAD_W, T)
        return vt[:SUBLANES, :LANES]

    def gate_chunk(w_ref, col0):
        def emit(j):
            cs = slice(j * NB, (j + 1) * NB)
            pre = jnp.dot(h, w_ref[:, cs], preferred_element_type=F32)
            gates_ref[:, col0 + j * NB:col0 + (j + 1) * NB] = jax.nn.sigmoid(pre)
            return pre[:SUBLANES, :LANES]
        return emit

    vt_ref[:, HEAD_W:, :] = jnp.ones((N_HEADS, VT_ROWS - HEAD_W, T), BF16)

    for j in range(n_chunks):
        glu_chunk(j)
    other_chunks = [(emit, j) for emit in (q_chunk, k_chunk, vt_chunk, gate_chunk(w_ga_ref, 0),
                                           gate_chunk(w_gb_ref, D_MODEL)) for j in range(n_chunks)]
    conv_steps = (D_MODEL // LANES) * (T // SUBLANES)
    steps_per_chunk = -(-conv_steps // len(other_chunks))
    previous = None
    for emit, j in other_chunks:
        tile = emit(j)
        if previous is not None:
            order_after.append(previous)
        run_conv(steps_per_chunk)
        previous = tile
    run_conv(conv_steps)

    rows = 32
    for r0 in range(0, T, rows):
        acc = y_ref[r0:r0 + rows, :]
        mu = jnp.mean(acc, axis=-1, keepdims=True)
        xc = acc - mu
        y = xc * lax.rsqrt(jnp.mean(xc * xc, axis=-1, keepdims=True) + EPS)
        y = y * lng_ref[...] + lnb_ref[...]
        cact_ref[r0:r0 + rows, :] = (y * jax.nn.sigmoid(y)).astype(BF16)


def _mix_in(x, mod_l, norm_g, w_mix, w_vt, w_dw, b_dw, ln_g, ln_b, *, layer):
    B, S, _ = x.shape
    nt = S // SEQ_TILE
    tile = pl.BlockSpec((None, SEQ_TILE, D_MODEL), lambda b, i: (b, i, 0))
    col_block = lambda j: _resident((None, D_MODEL, D_MODEL), (layer, 0, j))
    vec = lambda a: a.reshape(1, D_MODEL)
    w8 = jnp.repeat(w_dw, SUBLANES, axis=0)
    return pl.pallas_call(
        _mix_in_kernel,
        grid=(B, nt),
        in_specs=[
            tile,
            pl.BlockSpec((None, N_MOD, D_MODEL), lambda b, i: (b, 0, 0)),
            _resident((1, D_MODEL)),
            col_block(0),
            col_block(1),
            _resident((ATTN_W, D_MODEL)),
            col_block(3),
            col_block(4),
            col_block(5),
            col_block(6),
            _resident((CONV_K * SUBLANES, D_MODEL)),
            _resident((1, D_MODEL)),
            _resident((1, D_MODEL)),
            _resident((1, D_MODEL)),
        ],
        out_specs=[
            tile,
            tile,
            pl.BlockSpec((None, None, N_HEADS, VT_ROWS, SEQ_TILE), lambda b, i: (b, i, 0, 0, 0)),
            tile,
            pl.BlockSpec((None, SEQ_TILE, 2 * D_MODEL), lambda b, i: (b, i, 0)),
        ],
        out_shape=[
            jax.ShapeDtypeStruct((B, S, ATTN_W), BF16),
            jax.ShapeDtypeStruct((B, S, ATTN_W), BF16),
            jax.ShapeDtypeStruct((B, nt, N_HEADS, VT_ROWS, SEQ_TILE), BF16),
            jax.ShapeDtypeStruct((B, S, D_MODEL), BF16),
            jax.ShapeDtypeStruct((B, S, 2 * D_MODEL), F32),
        ],
        scratch_shapes=[
            pltpu.VMEM((SEQ_TILE + CONV_HALO, D_MODEL), F32),
            pltpu.VMEM((CONV_HALO, D_MODEL), F32),
            pltpu.VMEM((SEQ_TILE, D_MODEL), F32),
        ],
        compiler_params=_params("parallel", "arbitrary"),
        name="mix_in",
    )(x, mod_l, norm_g.reshape(1, D_MODEL), w_mix, w_mix, w_vt, w_mix, w_mix, w_mix, w_mix,
      w8, vec(b_dw), vec(ln_g), vec(ln_b))


def _bias_kernel(table_ref, o_ref):
    h = pl.program_id(0)
    t = pl.program_id(1)
    shape = (KV_TILE, SEQ_TILE)
    dist = (lax.broadcasted_iota(jnp.int32, shape, 1) - lax.broadcasted_iota(jnp.int32, shape, 0)
            + (t - 1) * KV_TILE)
    n = jnp.maximum(dist, 0)
    max_exact = NUM_BUCKETS // 2
    nf = jnp.maximum(n, 1).astype(F32)
    large = max_exact + (jnp.log(nf / max_exact) / math.log(MAX_DISTANCE / max_exact)
                         * (NUM_BUCKETS - max_exact)).astype(jnp.int32)
    large = jnp.minimum(large, NUM_BUCKETS - 1)
    bucket = jnp.where(n < max_exact, n, large)
    far = table_ref[h, NUM_BUCKETS - 1]
    val = jnp.zeros(shape, F32)
    for b in range(NUM_BUCKETS - 1):
        val = jnp.where(bucket == b, (table_ref[h, b] - far) * LOG2E, val)
    o_ref[...] = jnp.where(dist >= 0, val, MASK_BIAS)


def _bias_tiles(rel_bias):
    return pl.pallas_call(
        _bias_kernel,
        grid=(N_HEADS, N_BIAS_TILES),
        in_specs=[pl.BlockSpec(memory_space=pltpu.SMEM)],
        out_specs=pl.BlockSpec((None, None, KV_TILE, SEQ_TILE), lambda h, t: (h, t, 0, 0)),
        out_shape=jax.ShapeDtypeStruct((N_HEADS, N_BIAS_TILES, KV_TILE, SEQ_TILE), F32),
        compiler_params=_params("arbitrary", "arbitrary"),
        name="rel_bias_tiles",
    )(rel_bias)


def _attn_kernel(q_ref, k_ref, vt_ref, bias_ref, lamv_ref, sg_ref, o_ref,
                 qs_ref, s0_ref, s1_ref, mb0_ref, mb1_ref, p0_ref, p1_ref, al0_ref, al1_ref,
                 m_ref, acc_ref, *, lam_init):
    T = SEQ_TILE
    TK = KV_TILE
    per_tile = T // TK
    qi = pl.program_id(2)
    last_kb = per_tile * qi + (per_tile - 1)

    q = q_ref[...]
    lane = lax.broadcasted_iota(jnp.int32, q.shape, 1)
    zero = jnp.zeros_like(q)
    qs_ref[0:T, :] = jnp.where(lane < HEAD_DIM, q, zero)
    qs_ref[T:2 * T, :] = jnp.where(lane >= HEAD_DIM, q, zero)

    m_ref[...] = jnp.full(m_ref.shape, MASK_BIAS, F32)
    acc_ref[...] = jnp.zeros(acc_ref.shape, F32)

    n_cb = 2 * T // MXU_COLS

    def scores(kb, c, s_ref, mb_ref):
        kblk = k_ref[pl.ds(pl.multiple_of(kb * TK, TK), TK), :]
        s = lax.dot_general(kblk, qs_ref[c * MXU_COLS:(c + 1) * MXU_COLS, :], (((1,), (1,)), ((), ())),
                            preferred_element_type=F32)
        mx = None
        for r in range(0, TK, SUBLANES):
            chunk = s[r:r + SUBLANES, :]
            s_ref[c, r:r + SUBLANES, :] = chunk
            mx = chunk if mx is None else jnp.maximum(mx, chunk)
        mb_ref[c] = jnp.max(mx, axis=0, keepdims=True)

    def softmax(kb, c, s_ref, mb_ref, p_ref, al_ref, biased):
        s = s_ref[c]
        if biased:
            q0 = c * MXU_COLS % T
            s = s + bias_ref[jnp.clip(last_kb - kb, 0, N_BIAS_TILES - 1), :, q0:q0 + MXU_COLS]
            mb = jnp.max(s, axis=0, keepdims=True)
        else:
            mb = mb_ref[c]
        m_prev = m_ref[c]
        m_new = jnp.maximum(m_prev, mb)
        al_ref[c] = jnp.exp2(m_prev - m_new)
        m_ref[c] = m_new
        p_ref[c] = jnp.exp2(s - m_new).astype(BF16)

    def weighted_values(j, half, c, p_ref, al_ref):
        vt = vt_ref[j, :, half * TK:(half + 1) * TK]
        pv = jnp.dot(vt, p_ref[c], preferred_element_type=F32)
        acc_ref[c] = al_ref[c] * acc_ref[c] + pv

    def trip(j, biased):
        kb0 = per_tile * j
        for c in range(n_cb):
            scores(kb0 + 1, c, s1_ref, mb1_ref)
            softmax(kb0, c, s0_ref, mb0_ref, p0_ref, al0_ref, biased)
            weighted_values(j, 0, c, p0_ref, al0_ref)
        for c in range(n_cb):
            scores(jnp.minimum(kb0 + 2, last_kb), c, s0_ref, mb0_ref)
            softmax(kb0 + 1, c, s1_ref, mb1_ref, p1_ref, al1_ref, biased)
            weighted_values(j, 1, c, p1_ref, al1_ref)

    def far_trips(jj, carry):
        for u in range(FAR_UNROLL):
            trip(FAR_UNROLL * jj + u, False)
        return carry

    def far_trip(j, carry):
        trip(j, False)
        return carry

    for c in range(n_cb):
        scores(0, c, s0_ref, mb0_ref)
    n_far = jnp.maximum(qi - 1, 0)
    n_unrolled = n_far // FAR_UNROLL
    lax.fori_loop(0, n_unrolled, far_trips, 0)
    lax.fori_loop(FAR_UNROLL * n_unrolled, n_far, far_trip, 0)

    @pl.when(qi >= 1)
    def _():
        trip(qi - 1, True)
        trip(qi, True)

    @pl.when(qi == 0)
    def _():
        trip(qi, True)

    lamv = lamv_ref[...]
    lam = (jnp.exp(jnp.sum(lamv[0:1] * lamv[1:2], axis=1, keepdims=True))
           - jnp.exp(jnp.sum(lamv[2:3] * lamv[3:4], axis=1, keepdims=True)) + lam_init)

    def normalised(c):
        return acc_ref[c, :HEAD_W, :] / acc_ref[c, HEAD_W:HEAD_W + 1, :]

    half_cb = n_cb // 2
    o = jnp.concatenate([normalised(c) - lam * normalised(c + half_cb) for c in range(half_cb)],
                        axis=1)
    o = o * lax.rsqrt(jnp.mean(o * o, axis=0, keepdims=True) + EPS) * sg_ref[...]
    o = o * (1 - lam_init)
    o_ref[...] = o.T.astype(BF16)


def _attention(q, k, vt, bias, lamv, subln_g, *, lam_init):
    B, S, _ = q.shape
    T = SEQ_TILE
    nt = S // T
    n_cb = 2 * T // MXU_COLS
    return pl.pallas_call(
        functools.partial(_attn_kernel, lam_init=lam_init),
        grid=(B, N_HEADS, nt),
        in_specs=[
            pl.BlockSpec((None, T, HEAD_W), lambda b, h, i: (b, i, h)),
            pl.BlockSpec((None, S, HEAD_W), lambda b, h, i: (b, 0, h)),
            pl.BlockSpec((None, nt, None, VT_ROWS, T), lambda b, h, i: (b, 0, h, 0, 0)),
            pl.BlockSpec((None, N_BIAS_TILES, KV_TILE, T), lambda b, h, i: (h, 0, 0, 0)),
            pl.BlockSpec((4, HEAD_DIM), lambda b, h, i: (0, 0)),
            pl.BlockSpec((HEAD_W, 1), lambda b, h, i: (0, 0)),
        ],
        out_specs=pl.BlockSpec((None, T, HEAD_W), lambda b, h, i: (b, i, h)),
        out_shape=jax.ShapeDtypeStruct((B, S, ATTN_W), BF16),
        scratch_shapes=[
            pltpu.VMEM((2 * T, HEAD_W), BF16),
            pltpu.VMEM((n_cb, KV_TILE, MXU_COLS), F32),
            pltpu.VMEM((n_cb, KV_TILE, MXU_COLS), F32),
            pltpu.VMEM((n_cb, 1, MXU_COLS), F32),
            pltpu.VMEM((n_cb, 1, MXU_COLS), F32),
            pltpu.VMEM((n_cb, KV_TILE, MXU_COLS), BF16),
            pltpu.VMEM((n_cb, KV_TILE, MXU_COLS), BF16),
            pltpu.VMEM((n_cb, 1, MXU_COLS), F32),
            pltpu.VMEM((n_cb, 1, MXU_COLS), F32),
            pltpu.VMEM((n_cb, 1, MXU_COLS), F32),
            pltpu.VMEM((n_cb, VT_ROWS, MXU_COLS), F32),
        ],
        compiler_params=_params("parallel", "parallel", "arbitrary"),
        name="diff_attn",
    )(q, k, vt, bias, lamv, subln_g.reshape(HEAD_W, 1))


def _mix_out_kernel(x_ref, mod_ref, o_ref, cact_ref, gates_ref, w_ab_ref, w_cb_ref, w_mo_ref, y_ref):
    y_attn = jnp.dot(o_ref[...], w_ab_ref[...], preferred_element_type=F32)
    y_conv = jnp.dot(cact_ref[...], w_cb_ref[...], preferred_element_type=F32)
    merged = gates_ref[:, :D_MODEL] * y_attn + gates_ref[:, D_MODEL:] * y_conv
    out = jnp.dot(merged.astype(BF16), w_mo_ref[...], preferred_element_type=F32)
    y_ref[...] = x_ref[...] + mod_ref[5:6, :] * out


def _mix_out(x, mod_l, o, cact, gates, w_ab, w_cb, w_mo, *, layer):
    B, S, _ = x.shape
    tile = pl.BlockSpec((None, SEQ_TILE, D_MODEL), lambda b, i: (b, i, 0))
    weight = _resident((None, D_MODEL, D_MODEL), (layer, 0, 0))
    return pl.pallas_call(
        _mix_out_kernel,
        grid=(B, S // SEQ_TILE),
        in_specs=[
            tile,
            pl.BlockSpec((None, N_MOD, D_MODEL), lambda b, i: (b, 0, 0)),
            tile,
            tile,
            pl.BlockSpec((None, SEQ_TILE, 2 * D_MODEL), lambda b, i: (b, i, 0)),
            weight,
            weight,
            weight,
        ],
        out_specs=tile,
        out_shape=jax.ShapeDtypeStruct(x.shape, F32),
        compiler_params=_params("parallel", "parallel"),
        name="mix_out",
    )(x, mod_l, o, cact, gates, w_ab, w_cb, w_mo)


def kernel(x, c, ffn1_norm, ffn1_w_in, ffn1_w_out, mix_norm, w_mix_in, lambda_q1, lambda_k1, lambda_q2, lambda_k2, subln_gain, w_attn_branch, conv_dw, conv_dw_bias, conv_ln_gain, conv_ln_bias, w_conv_branch, w_mix_out, ffn2_norm, ffn2_w_in, ffn2_w_out, ada_w, ada_b, rel_bias, final_norm):
    B, S, _ = x.shape
    assert S % SEQ_TILE == 0 and SEQ_TILE % CONV_HALO == 0 and CONV_HALO >= CONV_K - 1
    assert FF_CHUNKS[0][0] == 0 and FF_CHUNKS[-1][1] == D_FF
    assert SEQ_TILE == 2 * KV_TILE and KV_TILE >= MAX_DISTANCE

    mod_rows = 8
    c_pad = jnp.zeros((mod_rows, D_MODEL), F32).at[:B].set(c)
    mod = _modulation(c_pad, ada_w, ada_b)[:, :B].reshape(DEPTH, B, N_MOD, D_MODEL)
    bias = _bias_tiles(rel_bias)

    ffn1_in, ffn1_out = _to_bf16(ffn1_w_in), _to_bf16(ffn1_w_out)
    ffn2_in, ffn2_out = _to_bf16(ffn2_w_in), _to_bf16(ffn2_w_out)
    w_mix = _to_bf16(w_mix_in)
    w_ab, w_cb, w_mo = _to_bf16(w_attn_branch), _to_bf16(w_conv_branch), _to_bf16(w_mix_out)
    w_vt = jnp.swapaxes(w_mix[:, :, 2 * ATTN_W:3 * ATTN_W], 1, 2)

    for l in range(DEPTH):
        lam_init = 0.8 - 0.6 * math.exp(-0.3 * l)
        mod_l = mod[l]
        x = _ffn(x, mod_l, ffn1_norm[l], ffn1_in, ffn1_out, final_norm,
                 layer=l, mod_row=0, final=False)

        q, k, vt, cact, gates = _mix_in(x, mod_l, mix_norm[l], w_mix, w_vt[l], conv_dw[l], conv_dw_bias[l],
                                        conv_ln_gain[l], conv_ln_bias[l], layer=l)
        lamv = jnp.stack([lambda_q1[l], lambda_k1[l], lambda_q2[l], lambda_k2[l]]).astype(F32)
        o = _attention(q, k, vt, bias, lamv, subln_gain[l], lam_init=lam_init)
        x = _mix_out(x, mod_l, o, cact, gates, w_ab, w_cb, w_mo, layer=l)

        x = _ffn(x, mod_l, ffn2_norm[l], ffn2_in, ffn2_out, final_norm,
                 layer=l, mod_row=6, final=(l == DEPTH - 1))
    return x
```

```python
import functools
import math

import jax
import jax.numpy as jnp
from jax import lax
from jax.experimental import pallas as pl
from jax.experimental.pallas import tpu as pltpu

D_MODEL = 1024
DEPTH = 2
N_HEADS = 8
HEAD_DIM = 64
HEAD_W = 2 * HEAD_DIM
ATTN_W = N_HEADS * HEAD_W
CONV_K = 31
D_FF = 2816
NUM_BUCKETS = 32
MAX_DISTANCE = 128
N_MOD = 9
EPS = 1e-6

SUBLANES = 8
LANES = 128
MXU_COLS = 256

SEQ_TILE = 512
KV_TILE = 256
FAR_UNROLLS = (4, 2, 1)
N_BIAS_TILES = 4
LOG2E = math.log2(math.e)
BF16_SUBLANES = 16
VT_ROWS = HEAD_W + BF16_SUBLANES
CONV_GROUPS_PER_VISIT = 2
CONV_HALO = 32
FF_CHUNKS = ((0, 5 * MXU_COLS), (5 * MXU_COLS, D_FF))
CAST_BLOCK_BYTES = 6 * 1024 * 1024
MASK_BIAS = -1e30
VMEM_LIMIT = 56 * 1024 * 1024

F32 = jnp.float32
BF16 = jnp.bfloat16


def _params(*sem):
    return pltpu.CompilerParams(dimension_semantics=sem, vmem_limit_bytes=VMEM_LIMIT)


def _resident(shape, index=None):
    index = (0,) * len(shape) if index is None else index
    return pl.BlockSpec(shape, lambda *_: index, pipeline_mode=pl.Buffered(1))


def _cast_kernel(x_ref, o_ref):
    o_ref[...] = x_ref[...].astype(BF16)


def _to_bf16(w):
    L, R, C = w.shape
    rb = R
    while rb * C * 4 > CAST_BLOCK_BYTES and rb % (2 * BF16_SUBLANES) == 0:
        rb //= 2
    spec = pl.BlockSpec((None, rb, C), lambda l, i: (l, i, 0))
    return pl.pallas_call(
        _cast_kernel,
        grid=(L, R // rb),
        in_specs=[spec],
        out_specs=spec,
        out_shape=jax.ShapeDtypeStruct(w.shape, BF16),
        compiler_params=_params("parallel", "parallel"),
        name="to_bf16",
    )(w)


def _rms_norm(x, g):
    return x * lax.rsqrt(jnp.mean(x * x, axis=-1, keepdims=True) + EPS) * g


def _mod_kernel(c_ref, w_ref, b_ref, o_ref):
    c = c_ref[...]
    c_act = (c * jax.nn.sigmoid(c)).astype(BF16)
    o_ref[...] = jnp.dot(c_act, w_ref[...].astype(BF16), preferred_element_type=F32) + b_ref[...]


def _modulation(c_pad, ada_w, ada_b):
    rows = c_pad.shape[0]
    return pl.pallas_call(
        _mod_kernel,
        grid=(DEPTH, N_MOD),
        in_specs=[
            pl.BlockSpec((rows, D_MODEL), lambda l, j: (0, 0)),
            pl.BlockSpec((None, D_MODEL, D_MODEL), lambda l, j: (l, 0, j)),
            pl.BlockSpec((None, 1, D_MODEL), lambda l, j: (l, 0, j)),
        ],
        out_specs=pl.BlockSpec((None, rows, D_MODEL), lambda l, j: (l, 0, j)),
        out_shape=jax.ShapeDtypeStruct((DEPTH, rows, N_MOD * D_MODEL), F32),
        compiler_params=_params("arbitrary", "arbitrary"),
        name="adaln_mod",
    )(c_pad, ada_w, ada_b.reshape(DEPTH, 1, N_MOD * D_MODEL))


def _ffn_kernel(x_ref, mod_ref, g_ref, w_in_ref, w_out_ref, fg_ref, o_ref, *, mod_row, final):
    x = x_ref[...]
    shift = mod_ref[mod_row:mod_row + 1, :]
    scale = mod_ref[mod_row + 1:mod_row + 2, :]
    gate = mod_ref[mod_row + 2:mod_row + 3, :]
    h = (_rms_norm(x, g_ref[...]) * (1 + scale) + shift).astype(BF16)
    out = None
    for c0, c1 in FF_CHUNKS:
        g = jnp.dot(h, w_in_ref[:, c0:c1], preferred_element_type=F32)
        u = jnp.dot(h, w_in_ref[:, D_FF + c0:D_FF + c1], preferred_element_type=F32)
        act = (g * jax.nn.sigmoid(g) * u).astype(BF16)
        part = jnp.dot(act, w_out_ref[c0:c1, :], preferred_element_type=F32)
        out = part if out is None else out + part
    y = x + (0.5 * gate) * out
    if final:
        y = _rms_norm(y, fg_ref[...])
    o_ref[...] = y


def _ffn(x, mod_l, norm_g, w_in, w_out, final_g, *, layer, mod_row, final):
    B, S, _ = x.shape
    tile = pl.BlockSpec((None, SEQ_TILE, D_MODEL), lambda b, i: (b, i, 0))
    return pl.pallas_call(
        functools.partial(_ffn_kernel, mod_row=mod_row, final=final),
        grid=(B, S // SEQ_TILE),
        in_specs=[
            tile,
            pl.BlockSpec((None, N_MOD, D_MODEL), lambda b, i: (b, 0, 0)),
            _resident((1, D_MODEL)),
            _resident((None, D_MODEL, 2 * D_FF), (layer, 0, 0)),
            _resident((None, D_FF, D_MODEL), (layer, 0, 0)),
            _resident((1, D_MODEL)),
        ],
        out_specs=tile,
        out_shape=jax.ShapeDtypeStruct(x.shape, F32),
        compiler_params=_params("parallel", "parallel"),
        name="ffn",
    )(x, mod_l, norm_g.reshape(1, D_MODEL), w_in, w_out, final_g.reshape(1, D_MODEL))


def _zero_from(x):
    bits = lax.bitcast_convert_type(x, jnp.uint32)
    return lax.bitcast_convert_type((bits >> 16) >> 16, F32)


def _conv_units(zext_ref, w8_ref, b_ref, y_ref, order_after):
    T, C = y_ref.shape
    halo_groups = CONV_HALO // SUBLANES
    n_a = (CONV_K - 1) // SUBLANES + 1
    sub = lax.broadcasted_iota(jnp.int32, (SUBLANES, LANES), 0)
    lane_blocks = [slice(c0, c0 + LANES) for c0 in range(0, C, LANES)]

    def taps(cs):
        return [w8_ref[SUBLANES * j:SUBLANES * (j + 1), cs] for j in range(CONV_K)]

    def group(g, cs):
        return zext_ref[g * SUBLANES:(g + 1) * SUBLANES, cs]

    def partial_sums(zs, w):
        ys = []
        for b in range(SUBLANES):
            acc = None
            for a in range(n_a):
                s = SUBLANES * a + b
                if s < CONV_K:
                    t = w[CONV_K - 1 - s] * zs[a]
                    acc = t if acc is None else acc + t
            ys.append(acc)
        return ys

    rolled_prev = {}
    for cs in lane_blocks:
        ys = partial_sums([group(halo_groups - 1 - a, cs) for a in range(n_a)], taps(cs))
        rolled_prev[cs.start] = [pltpu.roll(ys[b], b, 0) for b in range(1, SUBLANES)]

    zero, zero_due = None, set()
    for g0 in range(0, T // SUBLANES, CONV_GROUPS_PER_VISIT):
        for cs in lane_blocks:
            w = taps(cs)
            zs = [group(g0 + halo_groups - a, cs) for a in range(1, n_a)]
            for g in range(g0, g0 + CONV_GROUPS_PER_VISIT):
                if order_after:
                    zero, zero_due = _zero_from(order_after.pop()), {b.start for b in lane_blocks}
                    while order_after:
                        zero = zero + _zero_from(order_after.pop())
                zs = [group(g + halo_groups, cs)] + zs[:n_a - 1]
                ys = partial_sums(zs, w)
                out = ys[0] + b_ref[:, cs]
                rolled = [pltpu.roll(ys[b], b, 0) for b in range(1, SUBLANES)]
                for b in range(1, SUBLANES):
                    out = out + jnp.where(sub >= b, rolled[b - 1], rolled_prev[cs.start][b - 1])
                rolled_prev[cs.start] = rolled
                if cs.start in zero_due:
                    out = out + zero
                    zero_due.discard(cs.start)
                y_ref[g * SUBLANES:(g + 1) * SUBLANES, cs] = out
                yield


def _mix_in_kernel(x_ref, mod_ref, g_ref, w_q_ref, w_k_ref, w_vt_ref, w_a_ref, w_g_ref, w_ga_ref, w_gb_ref,
                   w8_ref, cb_ref, lng_ref, lnb_ref,
                   q_ref, k_ref, vt_ref, cact_ref, gates_ref,
                   zext_ref, halo_ref, y_ref):
    T = SEQ_TILE
    NB = MXU_COLS
    n_chunks = D_MODEL // NB
    x = x_ref[...]
    shift = mod_ref[3:4, :]
    scale = mod_ref[4:5, :]
    h = (_rms_norm(x, g_ref[...]) * (1 + scale) + shift).astype(BF16)

    @pl.when(pl.program_id(1) == 0)
    def _():
        zext_ref[0:CONV_HALO, :] = jnp.zeros((CONV_HALO, D_MODEL), F32)

    @pl.when(pl.program_id(1) > 0)
    def _():
        zext_ref[0:CONV_HALO, :] = halo_ref[...]

    order_after = []
    conv = _conv_units(zext_ref, w8_ref, cb_ref, y_ref, order_after)

    def run_conv(n_steps):
        for _ in range(n_steps):
            next(conv, None)

    def glu_chunk(j):
        cs = slice(j * NB, (j + 1) * NB)
        a = jnp.dot(h, w_a_ref[:, cs], preferred_element_type=F32)
        g = jnp.dot(h, w_g_ref[:, cs], preferred_element_type=F32)
        zext_ref[CONV_HALO:, cs] = a * jax.nn.sigmoid(g)
        halo_ref[:, cs] = zext_ref[T:T + CONV_HALO, cs]
        return g[:SUBLANES, :LANES]

    def q_chunk(j):
        cs = slice(j * NB, (j + 1) * NB)
        q = jnp.dot(h, w_q_ref[:, cs], preferred_element_type=F32)
        q_ref[:, cs] = (q * (HEAD_DIM ** -0.5 * LOG2E)).astype(BF16)
        return q[:SUBLANES, :LANES]

    def k_chunk(j):
        cs = slice(j * NB, (j + 1) * NB)
        k = jnp.dot(h, w_k_ref[:, cs], preferred_element_type=F32)
        k_ref[:, cs] = k.astype(BF16)
        return k[:SUBLANES, :LANES]

    def vt_chunk(j):
        heads = NB // HEAD_W
        vt = lax.dot_general(w_vt_ref[j * NB:(j + 1) * NB, :], h, (((1,), (1,)), ((), ())),
                             preferred_element_type=F32)
        vt_ref[j * heads:(j + 1) * heads, :HEAD_W, :] = vt.astype(BF16).reshape(heads, HEAD_W, T)
        return vt[:SUBLANES, :LANES]

    def gate_chunk(w_ref, col0):
        def emit(j):
            cs = slice(j * NB, (j + 1) * NB)
            pre = jnp.dot(h, w_ref[:, cs], preferred_element_type=F32)
            gates_ref[:, col0 + j * NB:col0 + (j + 1) * NB] = jax.nn.sigmoid(pre)
            return pre[:SUBLANES, :LANES]
        return emit

    vt_ref[:, HEAD_W:, :] = jnp.ones((N_HEADS, VT_ROWS - HEAD_W, T), BF16)

    for j in range(n_chunks):
        glu_chunk(j)
    other_chunks = [(emit, j) for emit in (q_chunk, k_chunk, vt_chunk, gate_chunk(w_ga_ref, 0),
                                           gate_chunk(w_gb_ref, D_MODEL)) for j in range(n_chunks)]
    conv_steps = (D_MODEL // LANES) * (T // SUBLANES)
    steps_per_chunk = -(-conv_steps // len(other_chunks))
    previous = None
    for emit, j in other_chunks:
        tile = emit(j)
        if previous is not None:
            order_after.append(previous)
        run_conv(steps_per_chunk)
        previous = tile
    run_conv(conv_steps)

    rows = 32
    for r0 in range(0, T, rows):
        acc = y_ref[r0:r0 + rows, :]
        mu = jnp.mean(acc, axis=-1, keepdims=True)
        xc = acc - mu
        y = xc * lax.rsqrt(jnp.mean(xc * xc, axis=-1, keepdims=True) + EPS)
        y = y * lng_ref[...] + lnb_ref[...]
        cact_ref[r0:r0 + rows, :] = (y * jax.nn.sigmoid(y)).astype(BF16)


def _mix_in(x, mod_l, norm_g, w_mix, w_vt, w_dw, b_dw, ln_g, ln_b, *, layer):
    B, S, _ = x.shape
    nt = S // SEQ_TILE
    tile = pl.BlockSpec((None, SEQ_TILE, D_MODEL), lambda b, i: (b, i, 0))
    col_block = lambda j: _resident((None, D_MODEL, D_MODEL), (layer, 0, j))
    vec = lambda a: a.reshape(1, D_MODEL)
    w8 = jnp.repeat(w_dw, SUBLANES, axis=0)
    return pl.pallas_call(
        _mix_in_kernel,
        grid=(B, nt),
        in_specs=[
            tile,
            pl.BlockSpec((None, N_MOD, D_MODEL), lambda b, i: (b, 0, 0)),
            _resident((1, D_MODEL)),
            col_block(0),
            col_block(1),
            _resident((ATTN_W, D_MODEL)),
            col_block(3),
            col_block(4),
            col_block(5),
            col_block(6),
            _resident((CONV_K * SUBLANES, D_MODEL)),
            _resident((1, D_MODEL)),
            _resident((1, D_MODEL)),
            _resident((1, D_MODEL)),
        ],
        out_specs=[
            tile,
            tile,
            pl.BlockSpec((None, None, N_HEADS, VT_ROWS, SEQ_TILE), lambda b, i: (b, i, 0, 0, 0)),
            tile,
            pl.BlockSpec((None, SEQ_TILE, 2 * D_MODEL), lambda b, i: (b, i, 0)),
        ],
        out_shape=[
            jax.ShapeDtypeStruct((B, S, ATTN_W), BF16),
            jax.ShapeDtypeStruct((B, S, ATTN_W), BF16),
            jax.ShapeDtypeStruct((B, nt, N_HEADS, VT_ROWS, SEQ_TILE), BF16),
            jax.ShapeDtypeStruct((B, S, D_MODEL), BF16),
            jax.ShapeDtypeStruct((B, S, 2 * D_MODEL), F32),
        ],
        scratch_shapes=[
            pltpu.VMEM((SEQ_TILE + CONV_HALO, D_MODEL), F32),
            pltpu.VMEM((CONV_HALO, D_MODEL), F32),
            pltpu.VMEM((SEQ_TILE, D_MODEL), F32),
        ],
        compiler_params=_params("parallel", "arbitrary"),
        name="mix_in",
    )(x, mod_l, norm_g.reshape(1, D_MODEL), w_mix, w_mix, w_vt, w_mix, w_mix, w_mix, w_mix,
      w8, vec(b_dw), vec(ln_g), vec(ln_b))


def _bias_kernel(table_ref, o_ref):
    h = pl.program_id(0)
    t = pl.program_id(1)
    shape = (KV_TILE, SEQ_TILE)
    dist = (lax.broadcasted_iota(jnp.int32, shape, 1) - lax.broadcasted_iota(jnp.int32, shape, 0)
            + (t - 1) * KV_TILE)
    n = jnp.maximum(dist, 0)
    max_exact = NUM_BUCKETS // 2
    nf = jnp.maximum(n, 1).astype(F32)
    large = max_exact + (jnp.log(nf / max_exact) / math.log(MAX_DISTANCE / max_exact)
                         * (NUM_BUCKETS - max_exact)).astype(jnp.int32)
    large = jnp.minimum(large, NUM_BUCKETS - 1)
    bucket = jnp.where(n < max_exact, n, large)
    far = table_ref[h, NUM_BUCKETS - 1]
    val = jnp.zeros(shape, F32)
    for b in range(NUM_BUCKETS - 1):
        val = jnp.where(bucket == b, (table_ref[h, b] - far) * LOG2E, val)
    o_ref[...] = jnp.where(dist >= 0, val, MASK_BIAS)


def _bias_tiles(rel_bias):
    return pl.pallas_call(
        _bias_kernel,
        grid=(N_HEADS, N_BIAS_TILES),
        in_specs=[pl.BlockSpec(memory_space=pltpu.SMEM)],
        out_specs=pl.BlockSpec((None, None, KV_TILE, SEQ_TILE), lambda h, t: (h, t, 0, 0)),
        out_shape=jax.ShapeDtypeStruct((N_HEADS, N_BIAS_TILES, KV_TILE, SEQ_TILE), F32),
        compiler_params=_params("arbitrary", "arbitrary"),
        name="rel_bias_tiles",
    )(rel_bias)


def _attn_kernel(q_ref, k_ref, vt_ref, bias_ref, lamv_ref, sg_ref, o_ref,
                 qs_ref, s0_ref, s1_ref, mb0_ref, mb1_ref, p0_ref, p1_ref, al0_ref, al1_ref,
                 m_ref, acc_ref, *, lam_init):
    T = SEQ_TILE
    TK = KV_TILE
    per_tile = T // TK
    qi = pl.program_id(2)
    last_kb = per_tile * qi + (per_tile - 1)

    q = q_ref[...]
    lane = lax.broadcasted_iota(jnp.int32, q.shape, 1)
    zero = jnp.zeros_like(q)
    qs_ref[0:T, :] = jnp.where(lane < HEAD_DIM, q, zero)
    qs_ref[T:2 * T, :] = jnp.where(lane >= HEAD_DIM, q, zero)

    m_ref[...] = jnp.full(m_ref.shape, MASK_BIAS, F32)
    acc_ref[...] = jnp.zeros(acc_ref.shape, F32)

    n_cb = 2 * T // MXU_COLS

    def scores(kb, c, s_ref, mb_ref, biased):
        kblk = k_ref[pl.ds(pl.multiple_of(kb * TK, TK), TK), :]
        s = lax.dot_general(kblk, qs_ref[c * MXU_COLS:(c + 1) * MXU_COLS, :], (((1,), (1,)), ((), ())),
                            preferred_element_type=F32)
        tile = jnp.clip(last_kb - kb, 0, N_BIAS_TILES - 1)
        q0 = c * MXU_COLS % T
        mx = None
        for r in range(0, TK, SUBLANES):
            chunk = s[r:r + SUBLANES, :]
            if biased:
                chunk = chunk + bias_ref[tile, r:r + SUBLANES, q0:q0 + MXU_COLS]
            s_ref[c, r:r + SUBLANES, :] = chunk
            mx = chunk if mx is None else jnp.maximum(mx, chunk)
        mb_ref[c] = jnp.max(mx, axis=0, keepdims=True)

    def softmax(c, s_ref, mb_ref, p_ref, al_ref):
        m_prev = m_ref[c]
        m_new = jnp.maximum(m_prev, mb_ref[c])
        al_ref[c] = jnp.exp2(m_prev - m_new)
        m_ref[c] = m_new
        p_ref[c] = jnp.exp2(s_ref[c] - m_new).astype(BF16)

    def weighted_values(j, half, c, p_ref, al_ref):
        vt = vt_ref[j, :, half * TK:(half + 1) * TK]
        pv = jnp.dot(vt, p_ref[c], preferred_element_type=F32)
        acc_ref[c] = al_ref[c] * acc_ref[c] + pv

    def trip(j, biased):
        kb0 = per_tile * j
        for c in range(n_cb):
            scores(kb0 + 1, c, s1_ref, mb1_ref, biased)
            softmax(c, s0_ref, mb0_ref, p0_ref, al0_ref)
            weighted_values(j, 0, c, p0_ref, al0_ref)
        for c in range(n_cb):
            scores(jnp.minimum(kb0 + 2, last_kb), c, s0_ref, mb0_ref, biased)
            softmax(c, s1_ref, mb1_ref, p1_ref, al1_ref)
            weighted_values(j, 1, c, p1_ref, al1_ref)


    for c in range(n_cb):
        scores(0, c, s0_ref, mb0_ref, True)
    n_far = jnp.maximum(qi - 1, 0)
    done = 0
    for unroll in FAR_UNROLLS:
        def far_trips(jj, carry, unroll=unroll, done=done):
            for u in range(unroll):
                trip(done + unroll * jj + u, False)
            return carry

        n_bodies = (n_far - done) // unroll
        lax.fori_loop(0, n_bodies, far_trips, 0)
        done = done + unroll * n_bodies

    @pl.when(qi >= 1)
    def _():
        trip(qi - 1, True)
        trip(qi, True)

    @pl.when(qi == 0)
    def _():
        trip(qi, True)

    lamv = lamv_ref[...]
    lam = (jnp.exp(jnp.sum(lamv[0:1] * lamv[1:2], axis=1, keepdims=True))
           - jnp.exp(jnp.sum(lamv[2:3] * lamv[3:4], axis=1, keepdims=True)) + lam_init)

    def normalised(c):
        return acc_ref[c, :HEAD_W, :] / acc_ref[c, HEAD_W:HEAD_W + 1, :]

    half_cb = n_cb // 2
    o = jnp.concatenate([normalised(c) - lam * normalised(c + half_cb) for c in range(half_cb)],
                        axis=1)
    o = o * lax.rsqrt(jnp.mean(o * o, axis=0, keepdims=True) + EPS) * sg_ref[...]
    o = o * (1 - lam_init)
    o_ref[...] = o.T.astype(BF16)


def _attention(q, k, vt, bias, lamv, subln_g, *, lam_init):
    B, S, _ = q.shape
    T = SEQ_TILE
    nt = S // T
    n_cb = 2 * T // MXU_COLS
    return pl.pallas_call(
        functools.partial(_attn_kernel, lam_init=lam_init),
        grid=(B, N_HEADS, nt),
        in_specs=[
            pl.BlockSpec((None, T, HEAD_W), lambda b, h, i: (b, i, h)),
            pl.BlockSpec((None, S, HEAD_W), lambda b, h, i: (b, 0, h)),
            pl.BlockSpec((None, nt, None, VT_ROWS, T), lambda b, h, i: (b, 0, h, 0, 0)),
            pl.BlockSpec((None, N_BIAS_TILES, KV_TILE, T), lambda b, h, i: (h, 0, 0, 0)),
            pl.BlockSpec((4, HEAD_DIM), lambda b, h, i: (0, 0)),
            pl.BlockSpec((HEAD_W, 1), lambda b, h, i: (0, 0)),
        ],
        out_specs=pl.BlockSpec((None, T, HEAD_W), lambda b, h, i: (b, i, h)),
        out_shape=jax.ShapeDtypeStruct((B, S, ATTN_W), BF16),
        scratch_shapes=[
            pltpu.VMEM((2 * T, HEAD_W), BF16),
            pltpu.VMEM((n_cb, KV_TILE, MXU_COLS), F32),
            pltpu.VMEM((n_cb, KV_TILE, MXU_COLS), F32),
            pltpu.VMEM((n_cb, 1, MXU_COLS), F32),
            pltpu.VMEM((n_cb, 1, MXU_COLS), F32),
            pltpu.VMEM((n_cb, KV_TILE, MXU_COLS), BF16),
            pltpu.VMEM((n_cb, KV_TILE, MXU_COLS), BF16),
            pltpu.VMEM((n_cb, 1, MXU_COLS), F32),
            pltpu.VMEM((n_cb, 1, MXU_COLS), F32),
            pltpu.VMEM((n_cb, 1, MXU_COLS), F32),
            pltpu.VMEM((n_cb, VT_ROWS, MXU_COLS), F32),
        ],
        compiler_params=_params("parallel", "parallel", "arbitrary"),
        name="diff_attn",
    )(q, k, vt, bias, lamv, subln_g.reshape(HEAD_W, 1))


def _mix_out_kernel(x_ref, mod_ref, o_ref, cact_ref, gates_ref, w_ab_ref, w_cb_ref, w_mo_ref, y_ref):
    y_attn = jnp.dot(o_ref[...], w_ab_ref[...], preferred_element_type=F32)
    y_conv = jnp.dot(cact_ref[...], w_cb_ref[...], preferred_element_type=F32)
    merged = gates_ref[:, :D_MODEL] * y_attn + gates_ref[:, D_MODEL:] * y_conv
    out = jnp.dot(merged.astype(BF16), w_mo_ref[...], preferred_element_type=F32)
    y_ref[...] = x_ref[...] + mod_ref[5:6, :] * out


def _mix_out(x, mod_l, o, cact, gates, w_ab, w_cb, w_mo, *, layer):
    B, S, _ = x.shape
    tile = pl.BlockSpec((None, SEQ_TILE, D_MODEL), lambda b, i: (b, i, 0))
    weight = _resident((None, D_MODEL, D_MODEL), (layer, 0, 0))
    return pl.pallas_call(
        _mix_out_kernel,
        grid=(B, S // SEQ_TILE),
        in_specs=[
            tile,
            pl.BlockSpec((None, N_MOD, D_MODEL), lambda b, i: (b, 0, 0)),
            tile,
            tile,
            pl.BlockSpec((None, SEQ_TILE, 2 * D_MODEL), lambda b, i: (b, i, 0)),
            weight,
            weight,
            weight,
        ],
        out_specs=tile,
        out_shape=jax.ShapeDtypeStruct(x.shape, F32),
        compiler_params=_params("parallel", "parallel"),
        name="mix_out",
    )(x, mod_l, o, cact, gates, w_ab, w_cb, w_mo)


def kernel(x, c, ffn1_norm, ffn1_w_in, ffn1_w_out, mix_norm, w_mix_in, lambda_q1, lambda_k1, lambda_q2, lambda_k2, subln_gain, w_attn_branch, conv_dw, conv_dw_bias, conv_ln_gain, conv_ln_bias, w_conv_branch, w_mix_out, ffn2_norm, ffn2_w_in, ffn2_w_out, ada_w, ada_b, rel_bias, final_norm):
    B, S, _ = x.shape
    assert S % SEQ_TILE == 0 and SEQ_TILE % CONV_HALO == 0 and CONV_HALO >= CONV_K - 1
    assert FF_CHUNKS[0][0] == 0 and FF_CHUNKS[-1][1] == D_FF
    assert SEQ_TILE == 2 * KV_TILE and KV_TILE >= MAX_DISTANCE

    mod_rows = 8
    c_pad = jnp.zeros((mod_rows, D_MODEL), F32).at[:B].set(c)
    mod = _modulation(c_pad, ada_w, ada_b)[:, :B].reshape(DEPTH, B, N_MOD, D_MODEL)
    bias = _bias_tiles(rel_bias)

    ffn1_in, ffn1_out = _to_bf16(ffn1_w_in), _to_bf16(ffn1_w_out)
    ffn2_in, ffn2_out = _to_bf16(ffn2_w_in), _to_bf16(ffn2_w_out)
    w_mix = _to_bf16(w_mix_in)
    w_ab, w_cb, w_mo = _to_bf16(w_attn_branch), _to_bf16(w_conv_branch), _to_bf16(w_mix_out)
    w_vt = jnp.swapaxes(w_mix[:, :, 2 * ATTN_W:3 * ATTN_W], 1, 2)

    for l in range(DEPTH):
        lam_init = 0.8 - 0.6 * math.exp(-0.3 * l)
        mod_l = mod[l]
        x = _ffn(x, mod_l, ffn1_norm[l], ffn1_in, ffn1_out, final_norm,
                 layer=l, mod_row=0, final=False)

        q, k, vt, cact, gates = _mix_in(x, mod_l, mix_norm[l], w_mix, w_vt[l], conv_dw[l], conv_dw_bias[l],
                                        conv_ln_gain[l], conv_ln_bias[l], layer=l)
        lamv = jnp.stack([lambda_q1[l], lambda_k1[l], lambda_q2[l], lambda_k2[l]]).astype(F32)
        o = _attention(q, k, vt, bias, lamv, subln_gain[l], lam_init=lam_init)
        x = _mix_out(x, mod_l, o, cact, gates, w_ab, w_cb, w_mo, layer=l)

        x = _ffn(x, mod_l, ffn2_norm[l], ffn2_in, ffn2_out, final_norm,
                 layer=l, mod_row=6, final=(l == DEPTH - 1))
    return x
```
